```python
import math
import jax, jax.numpy as jnp
from jax import lax
import numpy as np

D_MODEL = 4096
BATCH = 4
SEQ = 2048
DEPTH = 2
DEC_BATCH = 8
DEC_SEQ = 1
PAST_LEN = 16384
PAGE_SIZE = 128

M_HEADS = 4
M_DK = 256
M_DV = 512
M_CHUNK = 64
A_HEADS = 8
A_DH = 128
A_BRANCHES = ((128, 1), (512, 4), (2048, 16))
C_CH = 1024
C_WIDTH = 31
N_BUCKETS = 32
REL_MAX_DIST = 2048
XA_HEADS = 4
XA_DH = D_MODEL // XA_HEADS
N_MEM = 256
D_FF = 4 * D_MODEL
MIX_WIDTH = M_HEADS * M_DV + A_HEADS * A_DH + C_CH
IN_SIZES = (M_HEADS * M_DK, M_HEADS * M_DK, M_HEADS * M_DV, M_HEADS * M_DV, M_HEADS, M_HEADS) + (A_HEADS * A_DH,) * 9 + (C_CH, C_CH)
IN_COLS = sum(IN_SIZES)
DEEPNORM_ALPHA = (2 * DEPTH) ** 0.25
DEEPNORM_BETA = (8 * DEPTH) ** -0.25
LN_EPS = 1e-5

kernel_name = 'hymba_mlstm_dilated_conformer_deepnorm_step'

F32 = jnp.float32


def _split_points():
    pts, acc = [], 0
    for s in IN_SIZES[:-1]:
        acc += s
        pts.append(acc)
    return pts


def layer_norm(x, g, b):
    xf = x.astype(F32)
    mu = jnp.mean(xf, axis=-1, keepdims=True)
    var = jnp.mean(jnp.square(xf - mu), axis=-1, keepdims=True)
    return ((xf - mu) * lax.rsqrt(var + LN_EPS) * g.astype(F32) + b.astype(F32)).astype(x.dtype)


def post_ln(x, sub, g, b):
    return layer_norm(DEEPNORM_ALPHA * x + sub, g, b)


def rel_bucket(dist):
    exact = N_BUCKETS // 2
    lg = jnp.log(jnp.maximum(dist, 1).astype(F32) / exact) / math.log(REL_MAX_DIST / exact)
    large = jnp.minimum(exact + (lg * (N_BUCKETS - exact)).astype(jnp.int32), N_BUCKETS - 1)
    return jnp.where(dist < exact, dist, large)


def branch_bias(table, window, dil):
    dist = dil * jnp.arange(window // dil + 1)
    return table[rel_bucket(dist)]


def mlstm_chunkwise(q, k, v, ig, fg, C0, n0, m0):
    B, S, H, _ = q.shape
    L = math.gcd(S, M_CHUNK)
    nc = S // L

    def chunks(t):
        return t.reshape((B, nc, L) + t.shape[2:]).swapaxes(0, 1)

    causal = jnp.tril(jnp.ones((L, L), dtype=bool))

    def step(carry, xs):
        C, n, m = carry
        qc, kc, vc, ic, lfc = xs
        b = jnp.cumsum(lfc, axis=1)
        D = b[:, :, None, :] - b[:, None, :, :] + ic[:, None, :, :]
        D = jnp.where(causal[None, :, :, None], D, -jnp.inf)
        inter = b + m[:, None, :]
        mt = jnp.maximum(inter, jnp.max(D, axis=2))
        w_intra = jnp.exp(D - mt[:, :, None, :])
        w_inter = jnp.exp(inter - mt)
        sc = jnp.einsum('bqhd,bshd->bqsh', qc, kc) * w_intra
        num = w_inter[..., None] * jnp.einsum('bqhd,bhde->bqhe', qc, C) + jnp.einsum('bqsh,bshe->bqhe', sc, vc)
        den = w_inter * jnp.einsum('bqhd,bhd->bqh', qc, n) + jnp.sum(sc, axis=2)
        h = num / jnp.maximum(jnp.abs(den), jnp.exp(-mt))[..., None]
        m_new = mt[:, -1]
        w_end = jnp.exp(b[:, -1:] - b + ic - m_new[:, None])
        decay = jnp.exp(b[:, -1] + m - m_new)
        C_new = decay[..., None, None] * C + jnp.einsum('bsh,bshd,bshe->bhde', w_end, kc, vc)
        n_new = decay[..., None] * n + jnp.einsum('bsh,bshd->bhd', w_end, kc)
        return (C_new, n_new, m_new), h

    xs = (chunks(q), chunks(k), chunks(v), chunks(ig), chunks(jax.nn.log_sigmoid(fg)))
    (C1, n1, m1), h = lax.scan(step, (C0, n0, m0), xs)
    h = h.swapaxes(0, 1).reshape(B, S, H, v.shape[-1])
    return h, C1, n1, m1


def dilated_attn_prompt(q, k, v, bias_j, dil):
    B, S, H, Dh = q.shape
    J = bias_j.shape[0] - 1
    span = dil * J
    Sp = -(-S // span) * span
    nb = Sp // span

    def to_res(t):
        t = jnp.pad(t, ((0, 0), (0, Sp - S), (0, 0), (0, 0)))
        t = t.reshape(B, Sp // dil, dil, H, Dh).transpose(0, 2, 1, 3, 4)
        return t.reshape(B, dil, nb, J, H, Dh)

    def band(t):
        prev = jnp.concatenate([jnp.zeros_like(t[:, :, :1]), t[:, :, :-1]], axis=2)
        return jnp.concatenate([prev, t], axis=3)

    qr = to_res(q * A_DH ** -0.5)
    kb, vb = band(to_res(k)), band(to_res(v))
    qi = jnp.arange(J)[:, None]
    ki = jnp.arange(2 * J)[None, :]
    rel = qi + J - ki
    in_band = (rel >= 0) & (rel <= J)
    blk = jnp.arange(nb)[:, None, None]
    mask = in_band[None] & ~((blk == 0) & (ki < J)[None])
    bias = bias_j[jnp.clip(rel, 0, J)].transpose(2, 0, 1).astype(F32)
    logits = jnp.einsum('brnqhd,brnkhd->brnhqk', qr, kb).astype(F32) + bias
    logits = jnp.where(mask[None, None, :, None], logits, -jnp.inf)
    mx = jnp.max(logits, axis=-1)
    p = jnp.exp(logits - mx[..., None])
    o = jnp.einsum('brnhqk,brnkhd->brnqhd', p, vb.astype(F32))
    o = o.reshape(B, dil, Sp // dil, H, Dh).transpose(0, 2, 1, 3, 4).reshape(B, Sp, H, Dh)[:, :S]

    def back(t):
        t = t.transpose(0, 1, 2, 4, 3).reshape(B, dil, Sp // dil, H)
        return t.transpose(0, 2, 1, 3).reshape(B, Sp, H)[:, :S]

    return o, back(jnp.sum(p, axis=-1)), back(mx)


def dilated_attn_cached(q, kcat, vcat, bias_j, dil):
    B, T, H, Dh = q.shape
    J = bias_j.shape[0] - 1
    P = kcat.shape[1] - T
    idx = (P + jnp.arange(T))[:, None] - dil * jnp.arange(J + 1)[None, :]
    valid = idx >= 0
    idx = jnp.maximum(idx, 0)
    kg, vg = kcat[:, idx], vcat[:, idx]
    logits = jnp.einsum('bthd,btjhd->bhtj', q * A_DH ** -0.5, kg).astype(F32) + bias_j.T.astype(F32)[None, :, None, :]
    logits = jnp.where(valid[None, None], logits, -jnp.inf)
    mx = jnp.max(logits, axis=-1)
    p = jnp.exp(logits - mx[..., None])
    o = jnp.einsum('bhtj,btjhd->bthd', p, vg.astype(F32))
    return o, jnp.sum(p, axis=-1).transpose(0, 2, 1), mx.transpose(0, 2, 1)


def hybrid_mixer(h, win_past, C0, n0, m0, conv_past, rel_bias, w_in, gate_b, norm_g,
                 conv_w, conv_b, cln_g, cln_b, w_out, prompt):
    B, S, _ = h.shape
    parts = jnp.split(h @ w_in, _split_points(), axis=-1)
    mq, mk, mv, mo, mi, mf = parts[:6]
    a_parts = parts[6:15]
    c_val, c_gate = parts[15], parts[16]

    q = mq.reshape(B, S, M_HEADS, M_DK).astype(F32)
    k = mk.reshape(B, S, M_HEADS, M_DK).astype(F32) * M_DK ** -0.5
    v = mv.reshape(B, S, M_HEADS, M_DV).astype(F32)
    ig = mi.astype(F32) + gate_b[0].astype(F32)
    fg = mf.astype(F32) + gate_b[1].astype(F32)
    hm, C1, n1, m1 = mlstm_chunkwise(q, k, v, ig, fg, C0.astype(F32), n0.astype(F32), m0.astype(F32))
    hm = hm * lax.rsqrt(jnp.mean(jnp.square(hm), axis=-1, keepdims=True) + LN_EPS)
    hm = hm * norm_g.reshape(M_HEADS, M_DV).astype(F32)
    hm = (jax.nn.sigmoid(mo.reshape(B, S, M_HEADS, M_DV)) * hm.astype(h.dtype)).reshape(B, S, M_HEADS * M_DV)

    outs, dens, maxs, new_win = [], [], [], []
    for g, (window, dil) in enumerate(A_BRANCHES):
        qa, ka, va = (t.reshape(B, S, A_HEADS, A_DH) for t in a_parts[3 * g:3 * g + 3])
        bias_j = branch_bias(rel_bias[:, g * A_HEADS:(g + 1) * A_HEADS], window, dil)
        if prompt:
            o, s, mx = dilated_attn_prompt(qa, ka, va, bias_j, dil)
            keep = min(window, S)
            new_win += [ka[:, S - keep:], va[:, S - keep:]]
        else:
            kp, vp = win_past[2 * g], win_past[2 * g + 1]
            kcat = jnp.concatenate([kp.astype(ka.dtype), ka], axis=1)
            vcat = jnp.concatenate([vp.astype(va.dtype), va], axis=1)
            o, s, mx = dilated_attn_cached(qa, kcat, vcat, bias_j, dil)
            keep = kp.shape[1]
            new_win += [kcat[:, -keep:], vcat[:, -keep:]]
        outs.append(o)
        dens.append(s)
        maxs.append(mx)
    mxs = jnp.stack(maxs)
    wts = jnp.exp(mxs - jnp.max(mxs, axis=0))
    num = jnp.sum(wts[..., None] * jnp.stack(outs), axis=0)
    den = jnp.sum(wts * jnp.stack(dens), axis=0)
    ha = (num / den[..., None]).astype(h.dtype).reshape(B, S, A_HEADS * A_DH)

    glu = c_val * jax.nn.sigmoid(c_gate)
    xc = jnp.concatenate([conv_past.astype(glu.dtype), glu], axis=1)
    y = lax.conv_general_dilated(xc, conv_w[:, None, :].astype(glu.dtype), (1,), 'VALID',
                                 dimension_numbers=('NWC', 'WIO', 'NWC'), feature_group_count=C_CH) + conv_b
    y = jax.nn.silu(layer_norm(y, cln_g, cln_b))
    new_conv = xc[:, -(C_WIDTH - 1):]

    out = jnp.concatenate([hm, ha, y], axis=-1) @ w_out
    return out, tuple(new_win), C1, n1, m1, new_conv


def cross_attn(h, mk, mv, wq, wo):
    B, S, _ = h.shape
    q = (h @ wq).reshape(B, S, XA_HEADS, XA_DH) * XA_DH ** -0.5
    logits = jnp.einsum('bshd,bmhd->bhsm', q, mk.astype(q.dtype)).astype(F32)
    p = jax.nn.softmax(logits, axis=-1)
    o = jnp.einsum('bhsm,bmhd->bshd', p, mv.astype(F32)).astype(h.dtype)
    return o.reshape(B, S, D_MODEL) @ wo


def sq_relu_mlp(h, w1, w2):
    return jnp.square(jax.nn.relu(h @ w1)) @ w2


def setup_inputs(seed: int = 0) -> dict:
    key = jax.random.key(seed)
    ks = iter(jax.random.split(key, 40))

    def nrm(shape, scale=1.0):
        return scale * jax.random.normal(next(ks), shape, jnp.float32)

    win = [min(w, PAST_LEN) for w, _ in A_BRANCHES]
    gate_shift = jnp.array([0.0, 3.0], jnp.float32)[None, :, None]
    return {
        'x_prompt': nrm((BATCH, SEQ, D_MODEL)),
        'x_sample': nrm((DEC_BATCH, DEC_SEQ, D_MODEL)),
        'mem_prompt': nrm((BATCH, N_MEM, D_MODEL)),
        'cache_win_k1': nrm((DEPTH, DEC_BATCH, win[0], A_HEADS, A_DH)),
        'cache_win_v1': nrm((DEPTH, DEC_BATCH, win[0], A_HEADS, A_DH)),
        'cache_win_k2': nrm((DEPTH, DEC_BATCH, win[1], A_HEADS, A_DH)),
        'cache_win_v2': nrm((DEPTH, DEC_BATCH, win[1], A_HEADS, A_DH)),
        'cache_win_k3': nrm((DEPTH, DEC_BATCH, win[2], A_HEADS, A_DH)),
        'cache_win_v3': nrm((DEPTH, DEC_BATCH, win[2], A_HEADS, A_DH)),
        'state_mlstm_C': nrm((DEPTH, DEC_BATCH, M_HEADS, M_DK, M_DV), 0.1),
        'state_mlstm_n': nrm((DEPTH, DEC_BATCH, M_HEADS, M_DK), 0.1),
        'state_mlstm_m': nrm((DEPTH, DEC_BATCH, M_HEADS), 1.0),
        'state_conv': nrm((DEPTH, DEC_BATCH, C_WIDTH - 1, C_CH), 0.5),
        'cache_mem_k': nrm((DEPTH, DEC_BATCH, N_MEM, XA_HEADS, XA_DH)),
        'cache_mem_v': nrm((DEPTH, DEC_BATCH, N_MEM, XA_HEADS, XA_DH)),
        'rel_bias': nrm((N_BUCKETS, len(A_BRANCHES) * A_HEADS), 0.2),
        'w_in': nrm((DEPTH, D_MODEL, IN_COLS), D_MODEL ** -0.5),
        'mlstm_gate_bias': nrm((DEPTH, 2, M_HEADS), 0.1) + gate_shift,
        'mlstm_norm_g': 1.0 + nrm((DEPTH, M_HEADS * M_DV), 0.05),
        'conv_w': nrm((DEPTH, C_WIDTH, C_CH), C_WIDTH ** -0.5),
        'conv_b': nrm((DEPTH, C_CH), 0.02),
        'conv_ln_g': 1.0 + nrm((DEPTH, C_CH), 0.05),
        'conv_ln_b': nrm((DEPTH, C_CH), 0.02),
        'w_out': nrm((DEPTH, MIX_WIDTH, D_MODEL), DEEPNORM_BETA * MIX_WIDTH ** -0.5),
        'xa_wq': nrm((DEPTH, D_MODEL, D_MODEL), D_MODEL ** -0.5),
        'xa_wk': nrm((DEPTH, D_MODEL, D_MODEL), D_MODEL ** -0.5),
        'xa_wv': nrm((DEPTH, D_MODEL, D_MODEL), D_MODEL ** -0.5),
        'xa_wo': nrm((DEPTH, D_MODEL, D_MODEL), DEEPNORM_BETA * D_MODEL ** -0.5),
        'ffn_w1': nrm((DEPTH, D_MODEL, D_FF), D_MODEL ** -0.5),
        'ffn_w2': nrm((DEPTH, D_FF, D_MODEL), DEEPNORM_BETA * D_FF ** -0.5),
        'ln_g': 1.0 + nrm((DEPTH, 3, D_MODEL), 0.05),
        'ln_b': nrm((DEPTH, 3, D_MODEL), 0.02),
    }


def reference(x_prompt, x_sample, mem_prompt, cache_win_k1, cache_win_v1, cache_win_k2, cache_win_v2,
              cache_win_k3, cache_win_v3, state_mlstm_C, state_mlstm_n, state_mlstm_m, state_conv,
              cache_mem_k, cache_mem_v, rel_bias, w_in, mlstm_gate_bias, mlstm_norm_g, conv_w, conv_b,
              conv_ln_g, conv_ln_b, w_out, xa_wq, xa_wk, xa_wv, xa_wo, ffn_w1, ffn_w2, ln_g, ln_b):
    xp, xs = x_prompt, x_sample
    B = xp.shape[0]
    p_win, p_C, p_n, p_m, p_conv, p_mk, p_mv = [], [], [], [], [], [], []
    s_win, s_C, s_n, s_m, s_conv = [], [], [], [], []
    for l in range(DEPTH):
        mw = (w_in[l], mlstm_gate_bias[l], mlstm_norm_g[l], conv_w[l], conv_b[l], conv_ln_g[l], conv_ln_b[l], w_out[l])
        zC = jnp.zeros((B, M_HEADS, M_DK, M_DV), F32)
        zn = jnp.zeros((B, M_HEADS, M_DK), F32)
        zm = jnp.zeros((B, M_HEADS), F32)
        zconv = jnp.zeros((B, C_WIDTH - 1, C_CH), xp.dtype)
        mix, win_p, Cp, n_p, m_p, conv_p = hybrid_mixer(xp, None, zC, zn, zm, zconv, rel_bias, *mw, prompt=True)
        xp = post_ln(xp, mix, ln_g[l, 0], ln_b[l, 0])
        mk = (mem_prompt @ xa_wk[l]).reshape(B, N_MEM, XA_HEADS, XA_DH)
        mv = (mem_prompt @ xa_wv[l]).reshape(B, N_MEM, XA_HEADS, XA_DH)
        xp = post_ln(xp, cross_attn(xp, mk, mv, xa_wq[l], xa_wo[l]), ln_g[l, 1], ln_b[l, 1])
        xp = post_ln(xp, sq_relu_mlp(xp, ffn_w1[l], ffn_w2[l]), ln_g[l, 2], ln_b[l, 2])
        p_win.append(win_p); p_C.append(Cp); p_n.append(n_p); p_m.append(m_p); p_conv.append(conv_p)
        p_mk.append(mk); p_mv.append(mv)
        past = (cache_win_k1[l], cache_win_v1[l], cache_win_k2[l], cache_win_v2[l], cache_win_k3[l], cache_win_v3[l])
        mix, win_s, Cs, n_s, m_s, conv_s = hybrid_mixer(xs, past, state_mlstm_C[l], state_mlstm_n[l], state_mlstm_m[l],
                                                        state_conv[l], rel_bias, *mw, prompt=False)
        xs = post_ln(xs, mix, ln_g[l, 0], ln_b[l, 0])
        xs = post_ln(xs, cross_attn(xs, cache_mem_k[l], cache_mem_v[l], xa_wq[l], xa_wo[l]), ln_g[l, 1], ln_b[l, 1])
        xs = post_ln(xs, sq_relu_mlp(xs, ffn_w1[l], ffn_w2[l]), ln_g[l, 2], ln_b[l, 2])
        s_win.append(win_s); s_C.append(Cs); s_n.append(n_s); s_m.append(m_s); s_conv.append(conv_s)
    pw = [jnp.stack([w[i] for w in p_win]) for i in range(6)]
    sw = [jnp.stack([w[i] for w in s_win]) for i in range(6)]
    return (xp, xs,
            pw[0], pw[1], pw[2], pw[3], pw[4], pw[5],
            jnp.stack(p_C), jnp.stack(p_n), jnp.stack(p_m), jnp.stack(p_conv),
            jnp.stack(p_mk), jnp.stack(p_mv),
            sw[0], sw[1], sw[2], sw[3], sw[4], sw[5],
            jnp.stack(s_C), jnp.stack(s_n), jnp.stack(s_m), jnp.stack(s_conv))
```

```python
import functools
import math

import jax
import jax.numpy as jnp
from jax import lax
from jax.experimental import pallas as pl
from jax.experimental.pallas import tpu as pltpu

F32 = jnp.float32
BF16 = jnp.bfloat16

D_MODEL = 4096
DEPTH = 2
M_HEADS, M_DK, M_DV, M_CHUNK = 4, 256, 512, 64
A_HEADS, A_DH = 8, 128
A_BRANCHES = ((128, 1), (512, 4), (2048, 16))
A_J = 128
C_CH, C_WIDTH = 1024, 31
N_BUCKETS, REL_MAX_DIST = 32, 2048
XA_HEADS = 4
XA_DH = D_MODEL // XA_HEADS
N_MEM = 256
D_FF = 4 * D_MODEL
DEEPNORM_ALPHA = (2 * DEPTH) ** 0.25
LN_EPS = 1e-5

N_MAIN = 2 * M_HEADS * M_DK + 2 * M_HEADS * M_DV
N_ATT = 9 * A_HEADS * A_DH
N_PROJ = N_MAIN + N_ATT + 2 * C_CH
COL_BLK = 1024
GATE_PAD = 128
CONV_HALO = 32
LN_ROWS = 16
SAMPLE_ROWS = 16

VMEM_LIMIT = 56 * 1024 * 1024
NEG_INF = float("-inf")


def _cparams(sem):
    return pltpu.CompilerParams(dimension_semantics=sem, vmem_limit_bytes=VMEM_LIMIT)


def _layer_norm_rows(z, g, b):
    mu = jnp.mean(z, axis=-1, keepdims=True)
    zc = z - mu
    var = jnp.mean(zc * zc, axis=-1, keepdims=True)
    return zc * lax.rsqrt(var + LN_EPS) * g + b


def _mm_kernel(*refs, nk, epilogue, scale):
    if epilogue == "ln":
        x_ref, w_ref, res_ref, g_ref, b_ref, o32_ref, o16_ref = refs[:7]
        rest = refs[7:]
    else:
        x_ref, w_ref, o_ref = refs[:3]
        rest = refs[3:]

    def finish(acc):
        if epilogue == "ln":
            def rows(i, carry):
                r = pl.ds(pl.multiple_of(i * LN_ROWS, LN_ROWS), LN_ROWS)
                y = _layer_norm_rows(DEEPNORM_ALPHA * res_ref[r, :] + acc_ref[r, :], g_ref[...], b_ref[...])
                o32_ref[r, :] = y
                o16_ref[r, :] = y.astype(BF16)
                return carry
            lax.fori_loop(0, acc_ref.shape[0] // LN_ROWS, rows, 0)
        elif epilogue == "relu2":
            r = jnp.maximum(acc, 0.0)
            o_ref[...] = (r * r).astype(o_ref.dtype)
        else:
            o_ref[...] = (acc * scale if scale != 1.0 else acc).astype(o_ref.dtype)

    prod = jnp.dot(x_ref[...].astype(BF16), w_ref[...], preferred_element_type=F32)
    if nk == 1:
        finish(prod)
        return
    acc_ref = rest[0]
    k = pl.program_id(2)

    @pl.when(k == 0)
    def _():
        acc_ref[...] = prod

    @pl.when(k > 0)
    def _():
        acc_ref[...] += prod

    @pl.when(k == nk - 1)
    def _():
        finish(None if epilogue == "ln" else acc_ref[...])


def matmul(x, w, *, tm, tn, tk, out_dtype=F32, epilogue="none", scale=1.0, res=None, g=None, b=None):
    M, K = x.shape
    N = w.shape[1]
    tm, tn, tk = min(tm, M), min(tn, N), min(tk, K)
    assert M % tm == 0 and N % tn == 0 and K % tk == 0, (x.shape, w.shape, tm, tn, tk)
    nk = K // tk
    grid = (M // tm, N // tn, nk)
    in_specs = [pl.BlockSpec((tm, tk), lambda i, j, k: (i, k)),
                pl.BlockSpec((tk, tn), lambda i, j, k: (k, j))]
    args = [x, w]
    if epilogue == "ln":
        assert tn == N and nk > 1 and tm % LN_ROWS == 0
        in_specs += [pl.BlockSpec((tm, tn), lambda i, j, k: (i, j)),
                     pl.BlockSpec((1, tn), lambda i, j, k: (0, j)),
                     pl.BlockSpec((1, tn), lambda i, j, k: (0, j))]
        args += [res, g.reshape(1, N).astype(F32), b.reshape(1, N).astype(F32)]
        out_shape = (jax.ShapeDtypeStruct((M, N), F32), jax.ShapeDtypeStruct((M, N), BF16))
        out_specs = (pl.BlockSpec((tm, tn), lambda i, j, k: (i, j)),
                     pl.BlockSpec((tm, tn), lambda i, j, k: (i, j)))
    else:
        out_shape = jax.ShapeDtypeStruct((M, N), out_dtype)
        out_specs = pl.BlockSpec((tm, tn), lambda i, j, k: (i, j))
    scratch = [pltpu.VMEM((tm, tn), F32)] if nk > 1 else []
    return pl.pallas_call(
        functools.partial(_mm_kernel, nk=nk, epilogue=epilogue, scale=scale),
        grid=grid, in_specs=in_specs, out_specs=out_specs, out_shape=out_shape,
        scratch_shapes=scratch,
        compiler_params=_cparams(("parallel", "parallel", "arbitrary")),
        name=f"mm_{epilogue}",
    )(*args)


def _mlstm_kernel(gb_ref, g_ref, q_ref, k_ref, v_ref, o_ref, ng_ref, c0_ref, n0_ref, m0_ref,
                  h_ref, c_ref, n_ref, m_ref, *, L, nvalid):
    hd = pl.program_id(1)
    c = pl.program_id(2)

    @pl.when(c == 0)
    def _():
        c_ref[...] = c0_ref[...]
        n_ref[...] = n0_ref[...]
        m_ref[...] = m0_ref[...]

    ig = g_ref[0, 0, 0, pl.ds(c, 1), :] + gb_ref[0, hd]
    fg = g_ref[0, 1, 0, pl.ds(c, 1), :] + gb_ref[1, hd]
    lf = -(jnp.maximum(-fg, 0.0) + jnp.log1p(jnp.exp(-jnp.abs(fg))))
    if nvalid < L:
        lane = lax.broadcasted_iota(jnp.int32, (1, L), 1)
        ig = jnp.where(lane < nvalid, ig, NEG_INF)
        lf = jnp.where(lane < nvalid, lf, 0.0)
    qi = lax.broadcasted_iota(jnp.int32, (L, L), 0)
    si = lax.broadcasted_iota(jnp.int32, (L, L), 1)
    eye = qi == si
    tri = si <= qi
    lf_col = jnp.sum(jnp.where(eye, lf, 0.0), axis=1, keepdims=True)
    b_col = jnp.sum(jnp.where(tri, lf, 0.0), axis=1, keepdims=True)
    b_row = jnp.sum(jnp.where(qi <= si, lf_col, 0.0), axis=0, keepdims=True)
    m_prev = m_ref[0, 0]
    dmat = jnp.where(tri, b_col - b_row + ig, NEG_INF)
    inter = b_col + m_prev
    mt = jnp.maximum(inter, jnp.max(dmat, axis=1, keepdims=True))
    w_intra = jnp.exp(dmat - mt)
    w_inter = jnp.exp(inter - mt)

    q = q_ref[...]
    kf = k_ref[...] * (M_DK ** -0.5)
    qb, kb, vb = q.astype(BF16), kf.astype(BF16), v_ref[...].astype(BF16)
    c_prev = c_ref[0, 0]
    n_prev = n_ref[0, 0]
    sc = lax.dot_general(qb, kb, (((1,), (1,)), ((), ())), preferred_element_type=F32) * w_intra
    num = (w_inter * jnp.dot(qb, c_prev.astype(BF16), preferred_element_type=F32)
           + jnp.dot(sc.astype(BF16), vb, preferred_element_type=F32))
    den = w_inter * jnp.sum(q * n_prev, axis=1, keepdims=True) + jnp.sum(sc, axis=1, keepdims=True)
    h = num / jnp.maximum(jnp.abs(den), jnp.exp(-mt))

    m_new = mt[L - 1:L, :]
    b_last = b_col[L - 1:L, :]
    w_end = jnp.exp(b_last - b_row + ig - m_new)
    w_end_col = jnp.sum(jnp.where(eye, w_end, 0.0), axis=1, keepdims=True)
    decay = jnp.exp(b_last + m_prev - m_new)
    kw = kf * w_end_col
    c_ref[0, 0] = decay * c_prev + lax.dot_general(
        kw.astype(BF16), vb, (((0,), (0,)), ((), ())), preferred_element_type=F32)
    n_ref[0, 0] = decay * n_prev + jnp.sum(kw, axis=0, keepdims=True)
    m_ref[0, 0] = m_new

    hn = h * lax.rsqrt(jnp.mean(h * h, axis=-1, keepdims=True) + LN_EPS) * ng_ref[...]
    h_ref[...] = (jax.nn.sigmoid(o_ref[...]) * hn).astype(h_ref.dtype)


def mlstm(proj, gates, gate_b, norm_g, c0, n0, m0, *, B, S, L, nvalid):
    nc = S // L
    H, dk, dv = M_HEADS, M_DK, M_DV
    row = lambda b, h, c: b * nc + c
    st4 = lambda b, h, c: (b, h, 0, 0)
    return pl.pallas_call(
        functools.partial(_mlstm_kernel, L=L, nvalid=nvalid),
        grid=(B, H, nc),
        in_specs=[
            pl.BlockSpec(memory_space=pltpu.SMEM),
            pl.BlockSpec((1, 2, 1, nc, L), lambda b, h, c: (b, 0, h, 0, 0)),
            pl.BlockSpec((L, dk), lambda b, h, c: (row(b, h, c), h)),
            pl.BlockSpec((L, dk), lambda b, h, c: (row(b, h, c), H + h)),
            pl.BlockSpec((L, dv), lambda b, h, c: (row(b, h, c), H + h)),
            pl.BlockSpec((L, dv), lambda b, h, c: (row(b, h, c), 2 * H + h)),
            pl.BlockSpec((1, dv), lambda b, h, c: (0, h)),
            pl.BlockSpec((1, 1, dk, dv), st4),
            pl.BlockSpec((1, 1, 1, dk), st4),
            pl.BlockSpec((1, 1, 1, 1), st4),
        ],
        out_specs=(
            pl.BlockSpec((L, dv), lambda b, h, c: (row(b, h, c), h)),
            pl.BlockSpec((1, 1, dk, dv), st4),
            pl.BlockSpec((1, 1, 1, dk), st4),
            pl.BlockSpec((1, 1, 1, 1), st4),
        ),
        out_shape=(
            jax.ShapeDtypeStruct((B * S, H * dv), BF16),
            jax.ShapeDtypeStruct((B, H, dk, dv), F32),
            jax.ShapeDtypeStruct((B, H, 1, dk), F32),
            jax.ShapeDtypeStruct((B, H, 1, 1), F32),
        ),
        compiler_params=_cparams(("parallel", "parallel", "arbitrary")),
        name="mlstm",
    )(gate_b, gates, proj, proj, proj, proj, norm_g.reshape(1, H * dv),
      c0, n0.reshape(B, H, 1, dk), m0.reshape(B, H, 1, 1))


def _attn_prompt_kernel(*refs, first, last):
    bc_ref, bp_ref, q_ref, kc_ref, vc_ref, kp_ref, vp_ref = refs[:7]
    refs = refs[7:]
    if not first:
        num_in_ref, sm_in_ref = refs[:2]
        refs = refs[2:]
    if last:
        (out_ref,) = refs
    else:
        num_out_ref, sm_out_ref = refs
    n = pl.program_id(2)
    has_prev = n > 0
    T = q_ref.shape[1]
    lane = lax.broadcasted_iota(jnp.int32, (T, 128), 1)
    nt = (((1,), (1,)), ((), ()))
    sm_new = jnp.zeros((T, 128), F32)
    for h in range(A_HEADS):
        hs = slice(h * A_DH, (h + 1) * A_DH)
        qh = (q_ref[0, :, hs] * (A_DH ** -0.5)).astype(BF16)
        lc = lax.dot_general(qh, kc_ref[0, :, hs].astype(BF16), nt, preferred_element_type=F32) + bc_ref[h]
        lp = lax.dot_general(qh, kp_ref[0, :, hs].astype(BF16), nt, preferred_element_type=F32) + bp_ref[h]
        lp = jnp.where(has_prev, lp, NEG_INF)
        mx = jnp.maximum(jnp.max(lc, axis=1, keepdims=True), jnp.max(lp, axis=1, keepdims=True))
        pc = jnp.exp(lc - mx)
        pp = jnp.exp(lp - mx)
        s = jnp.sum(pc, axis=1, keepdims=True) + jnp.sum(pp, axis=1, keepdims=True)
        o = (jnp.dot(pc.astype(BF16), vc_ref[0, :, hs].astype(BF16), preferred_element_type=F32)
             + jnp.dot(pp.astype(BF16), vp_ref[0, :, hs].astype(BF16), preferred_element_type=F32))
        if not first:
            s_old = sm_in_ref[0, :, h:h + 1]
            m_old = sm_in_ref[0, :, A_HEADS + h:A_HEADS + h + 1]
            m_all = jnp.maximum(m_old, mx)
            a_old = jnp.exp(m_old - m_all)
            a_new = jnp.exp(mx - m_all)
            o = a_old * num_in_ref[0, :, hs] + a_new * o
            s = a_old * s_old + a_new * s
            mx = m_all
        if last:
            out_ref[0, :, hs] = (o / s).astype(out_ref.dtype)
        else:
            num_out_ref[0, :, hs] = o
            sm_new = jnp.where(lane == h, s, sm_new)
            sm_new = jnp.where(lane == A_HEADS + h, mx, sm_new)
    if not last:
        sm_out_ref[0] = sm_new


def attn_prompt_branch(proj, bias_c, bias_p, state, *, B, S, g, dil, first, last):
    W = proj.shape[1]
    Sr = S // dil
    T = A_J
    nb = Sr // T
    wb = W // COL_BLK
    qcol = N_MAIN // COL_BLK + 3 * g
    pv = proj.reshape(B, Sr, dil * W)
    cur = lambda off: pl.BlockSpec((1, T, COL_BLK), lambda b, r, n: (b, n, r * wb + qcol + off))
    prev = lambda off: pl.BlockSpec((1, T, COL_BLK), lambda b, r, n: (b, jnp.maximum(n - 1, 0), r * wb + qcol + off))
    bias_spec = pl.BlockSpec((A_HEADS, T, T), lambda b, r, n: (0, 0, 0))
    num_spec = pl.BlockSpec((1, T, COL_BLK), lambda b, r, n: (b, n, r))
    sm_spec = pl.BlockSpec((1, T, 128), lambda b, r, n: (b, n, r))
    in_specs = [bias_spec, bias_spec, cur(0), cur(1), cur(2), prev(1), prev(2)]
    args = [bias_c, bias_p, pv, pv, pv, pv, pv]
    if not first:
        in_specs += [num_spec, sm_spec]
        args += [state[0].reshape(B, Sr, dil * COL_BLK), state[1].reshape(B, Sr, dil * 128)]
    if last:
        out_specs = num_spec
        out_shape = jax.ShapeDtypeStruct((B, Sr, dil * COL_BLK), BF16)
    else:
        out_specs = (num_spec, sm_spec)
        out_shape = (jax.ShapeDtypeStruct((B, Sr, dil * COL_BLK), F32),
                     jax.ShapeDtypeStruct((B, Sr, dil * 128), F32))
    out = pl.pallas_call(
        functools.partial(_attn_prompt_kernel, first=first, last=last),
        grid=(B, dil, nb), in_specs=in_specs, out_specs=out_specs, out_shape=out_shape,
        compiler_params=_cparams(("parallel", "parallel", "arbitrary")),
        name=f"attn_prompt_{g}",
    )(*args)
    if last:
        return out.reshape(B * S, COL_BLK)
    return out[0].reshape(B * S, COL_BLK), out[1].reshape(B * S, 128)


def _attn_decode_kernel(bias_ref, a_ref, k1_ref, v1_ref, k2_ref, v2_ref, k3_ref, v3_ref, out_ref):
    caches = ((k1_ref, v1_ref), (k2_ref, v2_ref), (k3_ref, v3_ref))
    nt = (((1,), (1,)), ((), ()))
    hw = A_HEADS * A_DH
    for h in range(A_HEADS):
        outs, dens, maxs = [], [], []
        for g in range(3):
            base = 3 * g * hw + h * A_DH
            qf = (a_ref[0, :, base:base + A_DH] * (A_DH ** -0.5)).astype(BF16)
            kn = a_ref[0, :, base + hw:base + hw + A_DH].astype(BF16).astype(F32)
            vn = a_ref[0, :, base + 2 * hw:base + 2 * hw + A_DH].astype(BF16).astype(F32)
            kc = caches[g][0][0, :, h * A_DH:(h + 1) * A_DH].astype(BF16)
            vc = caches[g][1][0, :, h * A_DH:(h + 1) * A_DH].astype(BF16)
            q8 = jnp.broadcast_to(qf, (16, A_DH))
            lc = lax.dot_general(q8, kc, nt, preferred_element_type=F32)[0:1, :] + bias_ref[g, h:h + 1, 0:A_J]
            ln = (jnp.sum(qf.astype(F32) * kn, axis=1, keepdims=True)
                  + bias_ref[g, h:h + 1, A_J:A_J + 1])
            mx = jnp.maximum(jnp.max(lc, axis=1, keepdims=True), ln)
            pc = jnp.exp(lc - mx)
            pn = jnp.exp(ln - mx)
            p8 = jnp.broadcast_to(pc.astype(BF16), (16, A_J))
            o = jnp.dot(p8, vc, preferred_element_type=F32)[0:1, :] + pn.astype(BF16).astype(F32) * vn
            outs.append(o)
            dens.append(jnp.sum(pc, axis=1, keepdims=True) + pn)
            maxs.append(mx)
        m_all = jnp.maximum(jnp.maximum(maxs[0], maxs[1]), maxs[2])
        wts = [jnp.exp(m - m_all) for m in maxs]
        num = wts[0] * outs[0] + wts[1] * outs[1] + wts[2] * outs[2]
        den = wts[0] * dens[0] + wts[1] * dens[1] + wts[2] * dens[2]
        out_ref[0, :, h * A_DH:(h + 1) * A_DH] = num / den


def attn_decode(att, caches, bias_dec):
    B = att.shape[0]
    hw = A_HEADS * A_DH
    in_specs = [pl.BlockSpec(bias_dec.shape, lambda b: (0, 0, 0)),
                pl.BlockSpec((1, 1, N_ATT), lambda b: (b, 0, 0))]
    args = [bias_dec, att.reshape(B, 1, N_ATT)]
    for g, (window, dil) in enumerate(A_BRANCHES):
        for t in caches[2 * g:2 * g + 2]:
            assert t.shape[1] == window
            in_specs.append(pl.BlockSpec((1, A_J, hw), lambda b: (b, 0, 0)))
            args.append(t.reshape(B, window // dil, dil * hw))
    out = pl.pallas_call(
        _attn_decode_kernel, grid=(B,), in_specs=in_specs,
        out_specs=pl.BlockSpec((1, 1, hw), lambda b: (b, 0, 0)),
        out_shape=jax.ShapeDtypeStruct((B, 1, hw), F32),
        compiler_params=_cparams(("parallel",)),
        name="attn_decode",
    )(*args)
    return out.reshape(B, hw)


def _conv_kernel(cv_ref, cg_ref, hv_ref, hg_ref, past_ref, cw_ref, cb_ref, lg_ref, lb_ref,
                 y_ref, tail_ref, xs_ref, *, TS, RC, nvalid):
    t = pl.program_id(1)
    H = CONV_HALO

    @pl.when(t == 0)
    def _():
        xs_ref[0:H, :] = past_ref[0]

    @pl.when(t > 0)
    def _():
        xs_ref[0:H, :] = hv_ref[...] * jax.nn.sigmoid(hg_ref[...])

    xs_ref[H:H + TS, :] = cv_ref[...] * jax.nn.sigmoid(cg_ref[...])
    lead = H - (C_WIDTH - 1)
    for rc in range(TS // RC):
        acc = jnp.zeros((RC, C_CH), F32)
        for w in range(C_WIDTH):
            r0 = rc * RC + lead + w
            acc = acc + xs_ref[r0:r0 + RC, :] * cw_ref[w:w + 1, :]
        y = _layer_norm_rows(acc + cb_ref[...], lg_ref[...], lb_ref[...])
        y_ref[rc * RC:(rc + 1) * RC, :] = (y * jax.nn.sigmoid(y)).astype(y_ref.dtype)
    tail_ref[0] = xs_ref[nvalid:nvalid + H, :]


def conv_block(proj, past, conv_w, conv_b, ln_g, ln_b, *, B, S, TS, nvalid, vcol):
    nt = S // TS
    H = CONV_HALO
    RC = min(TS, 32)
    past_p = jnp.pad(past.astype(F32), ((0, 0), (H - (C_WIDTH - 1), 0), (0, 0)))
    cur = lambda col: pl.BlockSpec((TS, C_CH), lambda b, t: (b * nt + t, col))
    halo_rows = min(H, B * S)
    halo = lambda col: pl.BlockSpec(
        (halo_rows, C_CH), lambda b, t: (jnp.maximum((b * S + t * TS) // H - 1, 0), col))
    vec = pl.BlockSpec((1, C_CH), lambda b, t: (0, 0))
    y, tail = pl.pallas_call(
        functools.partial(_conv_kernel, TS=TS, RC=RC, nvalid=nvalid),
        grid=(B, nt),
        in_specs=[cur(vcol), cur(vcol + 1), halo(vcol), halo(vcol + 1),
                  pl.BlockSpec((1, H, C_CH), lambda b, t: (b, 0, 0)),
                  pl.BlockSpec((H, C_CH), lambda b, t: (0, 0)), vec, vec, vec],
        out_specs=(pl.BlockSpec((TS, C_CH), lambda b, t: (b * nt + t, 0)),
                   pl.BlockSpec((1, H, C_CH), lambda b, t: (b, 0, 0))),
        out_shape=(jax.ShapeDtypeStruct((B * S, C_CH), BF16),
                   jax.ShapeDtypeStruct((B, H, C_CH), F32)),
        scratch_shapes=[pltpu.VMEM((H + TS, C_CH), F32)],
        compiler_params=_cparams(("parallel", "arbitrary")),
        name="conv_ln_silu",
    )(proj, proj, proj, proj, past_p,
      jnp.pad(conv_w.astype(F32), ((0, H - C_WIDTH), (0, 0))),
      conv_b.reshape(1, C_CH), ln_g.reshape(1, C_CH), ln_b.reshape(1, C_CH))
    return y, tail[:, H - (C_WIDTH - 1):]


def _xattn_kernel(q_ref, mk_ref, mv_ref, o_ref):
    q = q_ref[0]
    logits = lax.dot_general(q, mk_ref[0].astype(BF16), (((1,), (1,)), ((), ())),
                             preferred_element_type=F32)
    e = jnp.exp(logits - jnp.max(logits, axis=-1, keepdims=True))
    p = e / jnp.sum(e, axis=-1, keepdims=True)
    o_ref[0] = jnp.dot(p.astype(BF16), mv_ref[0].astype(BF16), preferred_element_type=F32).astype(o_ref.dtype)


def cross_attn(q, mk, mv, *, TS):
    B, S, _ = q.shape
    TS = min(TS, S)
    return pl.pallas_call(
        _xattn_kernel, grid=(B, XA_HEADS, S // TS),
        in_specs=[pl.BlockSpec((1, TS, XA_DH), lambda b, h, t: (b, t, h)),
                  pl.BlockSpec((1, N_MEM, XA_DH), lambda b, h, t: (b, 0, h)),
                  pl.BlockSpec((1, N_MEM, XA_DH), lambda b, h, t: (b, 0, h))],
        out_specs=pl.BlockSpec((1, TS, XA_DH), lambda b, h, t: (b, t, h)),
        out_shape=jax.ShapeDtypeStruct(q.shape, BF16),
        compiler_params=_cparams(("parallel", "parallel", "arbitrary")),
        name="cross_attn",
    )(q, mk, mv)


def _rel_bucket(dist):
    exact = N_BUCKETS // 2
    lg = jnp.log(jnp.maximum(dist, 1).astype(F32) / exact) / math.log(REL_MAX_DIST / exact)
    large = jnp.minimum(exact + (lg * (N_BUCKETS - exact)).astype(jnp.int32), N_BUCKETS - 1)
    return jnp.where(dist < exact, dist, large)


def _bias_tables(rel_bias):
    J = A_J
    qi = jnp.arange(J)[:, None]
    ki = jnp.arange(J)[None, :]
    cur, prev, dec = [], [], []
    for g, (window, dil) in enumerate(A_BRANCHES):
        table = rel_bias[:, g * A_HEADS:(g + 1) * A_HEADS].astype(F32)
        bias_j = table[_rel_bucket(dil * jnp.arange(window // dil + 1))]
        bt = bias_j.T
        rel_c = qi - ki
        rel_p = qi + J - ki
        cur.append(jnp.where((rel_c >= 0)[None], bt[:, jnp.clip(rel_c, 0, J)], NEG_INF))
        prev.append(jnp.where((rel_p <= J)[None], bt[:, jnp.clip(rel_p, 0, J)], NEG_INF))
        row = jnp.concatenate([bt[:, J - jnp.arange(J)], bt[:, 0:1]], axis=1)
        dec.append(jnp.pad(row, ((0, 0), (0, 2 * J - (J + 1)))))
    return cur, prev, jnp.stack(dec)


def _prep_layer_weights(l, w_in, w_out, xa_wq, xa_wk, xa_wv, xa_wo, ffn_w1, ffn_w2):
    wi = w_in[l]
    n_gate = 2 * M_HEADS
    w_proj = jnp.concatenate([wi[:, :N_MAIN], wi[:, N_MAIN + n_gate:]], axis=1).astype(BF16)
    w_gate = jnp.pad(wi[:, N_MAIN:N_MAIN + n_gate], ((0, 0), (0, GATE_PAD - n_gate))).astype(BF16)
    cast = lambda w: w[l].astype(BF16)
    return dict(proj=w_proj, gate=w_gate, out=cast(w_out), wq=cast(xa_wq), wk=cast(xa_wk), wv=cast(xa_wv),
                wo=cast(xa_wo), w1=cast(ffn_w1), w2=cast(ffn_w2))


def _gate_layout(gmat, B, S, L):
    g = gmat[:, :2 * M_HEADS].reshape(B, S // L, L, 2, M_HEADS)
    return g.transpose(0, 3, 4, 1, 2)


def _trunk_tail(x32, x16, mix16, mk, mv, w, lg, lb, *, B, S, tm, xa_ts):
    x32, x16 = matmul(mix16, w["out"], tm=tm, tn=D_MODEL, tk=512, epilogue="ln", res=x32, g=lg[0], b=lb[0])
    q = matmul(x16, w["wq"], tm=1024, tn=1024, tk=D_MODEL, out_dtype=BF16, scale=XA_DH ** -0.5)
    o = cross_attn(q.reshape(B, S, D_MODEL), mk, mv, TS=xa_ts).reshape(B * S, D_MODEL)
    x32, x16 = matmul(o, w["wo"], tm=tm, tn=D_MODEL, tk=512, epilogue="ln", res=x32, g=lg[1], b=lb[1])
    hid = matmul(x16, w["w1"], tm=1024, tn=1024, tk=D_MODEL, out_dtype=BF16, epilogue="relu2")
    x32, x16 = matmul(hid, w["w2"], tm=tm, tn=D_MODEL, tk=512, epilogue="ln", res=x32, g=lg[2], b=lb[2])
    return x32, x16


def kernel(x_prompt, x_sample, mem_prompt, cache_win_k1, cache_win_v1, cache_win_k2, cache_win_v2, cache_win_k3, cache_win_v3, state_mlstm_C, state_mlstm_n, state_mlstm_m, state_conv, cache_mem_k, cache_mem_v, rel_bias, w_in, mlstm_gate_bias, mlstm_norm_g, conv_w, conv_b, conv_ln_g, conv_ln_b, w_out, xa_wq, xa_wk, xa_wv, xa_wo, ffn_w1, ffn_w2, ln_g, ln_b):
    B, S, D = x_prompt.shape
    BS = x_sample.shape[0]
    MS = SAMPLE_ROWS
    L = math.gcd(S, M_CHUNK)
    bias_c, bias_p, bias_dec = _bias_tables(rel_bias)
    cache_win = (cache_win_k1, cache_win_v1, cache_win_k2, cache_win_v2, cache_win_k3, cache_win_v3)

    xp32 = x_prompt.reshape(B * S, D)
    xp16 = xp32.astype(BF16)
    xs32 = jnp.pad(x_sample.reshape(BS, D), ((0, MS - BS), (0, 0)))
    xs16 = xs32.astype(BF16)
    mem16 = mem_prompt.reshape(B * N_MEM, D).astype(BF16)

    p_win = [[] for _ in range(6)]
    s_win = [[] for _ in range(6)]
    p_C, p_n, p_m, p_conv, p_mk, p_mv = [], [], [], [], [], []
    s_C, s_n, s_m, s_conv = [], [], [], []
    for l in range(DEPTH):
        w = _prep_layer_weights(l, w_in, w_out, xa_wq, xa_wk, xa_wv, xa_wo, ffn_w1, ffn_w2)
        gate_b = mlstm_gate_bias[l].astype(F32)

        proj = matmul(xp16, w["proj"], tm=1024, tn=1024, tk=D)
        gmat = matmul(xp16, w["gate"], tm=1024, tn=GATE_PAD, tk=D)
        hm, C1, n1, m1 = mlstm(proj, _gate_layout(gmat, B, S, L), gate_b, mlstm_norm_g[l],
                               jnp.zeros((B, M_HEADS, M_DK, M_DV), F32), jnp.zeros((B, M_HEADS, M_DK), F32),
                               jnp.zeros((B, M_HEADS), F32), B=B, S=S, L=L, nvalid=L)
        state = None
        for g, (window, dil) in enumerate(A_BRANCHES):
            state = attn_prompt_branch(proj, bias_c[g], bias_p[g], state, B=B, S=S, g=g, dil=dil,
                                       first=(g == 0), last=(g == len(A_BRANCHES) - 1))
            keep = min(window, S)
            c0 = N_MAIN + (3 * g + 1) * COL_BLK
            kv = proj.reshape(B, S, N_PROJ)[:, S - keep:, c0:c0 + 2 * COL_BLK]
            p_win[2 * g].append(kv[..., :COL_BLK].reshape(B, keep, A_HEADS, A_DH))
            p_win[2 * g + 1].append(kv[..., COL_BLK:].reshape(B, keep, A_HEADS, A_DH))
        ha = state
        yc, conv_tail = conv_block(proj, jnp.zeros((B, C_WIDTH - 1, C_CH), F32), conv_w[l], conv_b[l],
                                   conv_ln_g[l], conv_ln_b[l], B=B, S=S, TS=128, nvalid=128,
                                   vcol=(N_MAIN + N_ATT) // COL_BLK)
        mix = jnp.concatenate([hm, ha, yc], axis=1)
        mk = matmul(mem16, w["wk"], tm=1024, tn=1024, tk=D)
        mv = matmul(mem16, w["wv"], tm=1024, tn=1024, tk=D)
        xp32, xp16 = _trunk_tail(xp32, xp16, mix, mk.reshape(B, N_MEM, D), mv.reshape(B, N_MEM, D), w,
                                 ln_g[l], ln_b[l], B=B, S=S, tm=256, xa_ts=512)
        p_C.append(C1); p_n.append(n1.reshape(B, M_HEADS, M_DK)); p_m.append(m1.reshape(B, M_HEADS))
        p_conv.append(conv_tail)
        p_mk.append(mk.reshape(B, N_MEM, XA_HEADS, XA_DH)); p_mv.append(mv.reshape(B, N_MEM, XA_HEADS, XA_DH))

        sproj = matmul(xs16, w["proj"], tm=MS, tn=1024, tk=D)
        sgm = matmul(xs16, w["gate"], tm=MS, tn=GATE_PAD, tk=D)
        LS = M_CHUNK
        main_pad = jnp.zeros((BS, LS, N_MAIN), F32).at[:, 0].set(sproj[:BS, :N_MAIN]).reshape(BS * LS, N_MAIN)
        gate_pad = jnp.zeros((BS, LS, GATE_PAD), F32).at[:, 0].set(sgm[:BS]).reshape(BS * LS, GATE_PAD)
        shm, Cs, ns, ms = mlstm(main_pad, _gate_layout(gate_pad, BS, LS, LS), gate_b, mlstm_norm_g[l],
                                state_mlstm_C[l].astype(F32), state_mlstm_n[l].astype(F32),
                                state_mlstm_m[l].astype(F32), B=BS, S=LS, L=LS, nvalid=1)
        shm = shm.reshape(BS, LS, M_HEADS * M_DV)[:, 0]
        att = sproj[:BS, N_MAIN:N_MAIN + N_ATT]
        past = [c[l] for c in cache_win]
        sha = attn_decode(att, past, bias_dec)
        for g in range(len(A_BRANCHES)):
            for j in range(2):
                c0 = (3 * g + 1 + j) * COL_BLK
                new_row = att[:, c0:c0 + COL_BLK].reshape(BS, 1, A_HEADS, A_DH)
                s_win[2 * g + j].append(jnp.concatenate([past[2 * g + j][:, 1:], new_row], axis=1))
        CT = 8
        cpad = jnp.zeros((BS, CT, 2 * C_CH), F32).at[:, 0].set(sproj[:BS, N_MAIN + N_ATT:]).reshape(BS * CT, 2 * C_CH)
        syc, sconv_tail = conv_block(cpad, state_conv[l], conv_w[l], conv_b[l], conv_ln_g[l], conv_ln_b[l],
                                     B=BS, S=CT, TS=CT, nvalid=1, vcol=0)
        syc = syc.reshape(BS, CT, C_CH)[:, 0]
        smix = jnp.concatenate([shm, sha.astype(BF16), syc], axis=1)
        smix = jnp.pad(smix, ((0, MS - BS), (0, 0)))
        smk = cache_mem_k[l].reshape(BS, N_MEM, D)
        smv = cache_mem_v[l].reshape(BS, N_MEM, D)
        xs32, xs16 = _sample_tail(xs32, xs16, smix, smk, smv, w, ln_g[l], ln_b[l], BS=BS, MS=MS)
        s_C.append(Cs); s_n.append(ns.reshape(BS, M_HEADS, M_DK)); s_m.append(ms.reshape(BS, M_HEADS))
        s_conv.append(sconv_tail)

    st = lambda xs: jnp.stack(xs)
    return (xp32.reshape(B, S, D), xs32[:BS].reshape(BS, 1, D),
            st(p_win[0]), st(p_win[1]), st(p_win[2]), st(p_win[3]), st(p_win[4]), st(p_win[5]),
            st(p_C), st(p_n), st(p_m), st(p_conv), st(p_mk), st(p_mv),
            st(s_win[0]), st(s_win[1]), st(s_win[2]), st(s_win[3]), st(s_win[4]), st(s_win[5]),
            st(s_C), st(s_n), st(s_m), st(s_conv))


def _sample_tail(x32, x16, mix16, mk, mv, w, lg, lb, *, BS, MS):
    x32, x16 = matmul(mix16, w["out"], tm=MS, tn=D_MODEL, tk=512, epilogue="ln", res=x32, g=lg[0], b=lb[0])
    q = matmul(x16, w["wq"], tm=MS, tn=1024, tk=D_MODEL, out_dtype=BF16, scale=XA_DH ** -0.5)
    qs = jnp.zeros((BS, MS, D_MODEL), BF16).at[:, 0].set(q[:BS])
    o = cross_attn(qs, mk, mv, TS=MS)[:, 0]
    o = jnp.pad(o, ((0, MS - BS), (0, 0)))
    x32, x16 = matmul(o, w["wo"], tm=MS, tn=D_MODEL, tk=512, epilogue="ln", res=x32, g=lg[1], b=lb[1])
    hid = matmul(x16, w["w1"], tm=MS, tn=1024, tk=D_MODEL, out_dtype=BF16, epilogue="relu2")
    x32, x16 = matmul(hid, w["w2"], tm=MS, tn=D_MODEL, tk=512, epilogue="ln", res=x32, g=lg[2], b=lb[2])
    return x32, x16
```

```python
import functools
import math

import jax
import jax.numpy as jnp
from jax import lax
from jax.experimental import pallas as pl
from jax.experimental.pallas import tpu as pltpu

F32 = jnp.float32
BF16 = jnp.bfloat16

D_MODEL = 4096
DEPTH = 2
M_HEADS, M_DK, M_DV, M_CHUNK = 4, 256, 512, 64
A_HEADS, A_DH = 8, 128
A_BRANCHES = ((128, 1), (512, 4), (2048, 16))
A_J = 128
C_CH, C_WIDTH = 1024, 31
N_BUCKETS, REL_MAX_DIST = 32, 2048
XA_HEADS = 4
XA_DH = D_MODEL // XA_HEADS
N_MEM = 256
D_FF = 4 * D_MODEL
DEEPNORM_ALPHA = (2 * DEPTH) ** 0.25
LN_EPS = 1e-5

N_MAIN = 2 * M_HEADS * M_DK + 2 * M_HEADS * M_DV
N_GATE = 2 * M_HEADS
N_ATT = 9 * A_HEADS * A_DH
COL_BLK = 1024
CONV_COL = N_MAIN // COL_BLK
MIX_ATT_COL = M_HEADS * M_DV // COL_BLK
MIX_CONV_COL = MIX_ATT_COL + 1
GATE_PAD = 128
CONV_HALO = 32
LN_ROWS = 16
SAMPLE_ROWS = 16
COPY_ROWS = 64

VMEM_LIMIT = 56 * 1024 * 1024
NEG_INF = float("-inf")


def _cparams(sem):
    return pltpu.CompilerParams(dimension_semantics=sem, vmem_limit_bytes=VMEM_LIMIT)


def _layer_norm_rows(z, g, b):
    mu = jnp.mean(z, axis=-1, keepdims=True)
    zc = z - mu
    var = jnp.mean(zc * zc, axis=-1, keepdims=True)
    return zc * lax.rsqrt(var + LN_EPS) * g + b


def _mm_kernel(x_ref, w_ref, o_ref, *scratch, nk, epilogue, scale, head_major):
    def finish(acc):
        if epilogue == "relu2":
            r = jnp.maximum(acc, 0.0)
            acc = r * r
        elif scale != 1.0:
            acc = acc * scale
        if head_major:
            for j in range(o_ref.shape[0]):
                o_ref[j] = acc[:, j * 128:(j + 1) * 128].astype(o_ref.dtype)
        else:
            o_ref[...] = acc.astype(o_ref.dtype)

    prod = jnp.dot(x_ref[...].astype(BF16), w_ref[...], preferred_element_type=F32)
    if nk == 1:
        finish(prod)
        return
    (acc_ref,) = scratch
    k = pl.program_id(2)

    @pl.when(k == 0)
    def _():
        acc_ref[...] = prod

    @pl.when(k > 0)
    def _():
        acc_ref[...] += prod

    @pl.when(k == nk - 1)
    def _():
        finish(acc_ref[...])


def matmul(x, w, *, tm, tn, tk, out_dtype=F32, epilogue="none", scale=1.0, head_major=False):
    M, K = x.shape
    N = w.shape[1]
    tm, tn, tk = min(tm, M), min(tn, N), min(tk, K)
    assert M % tm == 0 and N % tn == 0 and K % tk == 0, (x.shape, w.shape, tm, tn, tk)
    nk = K // tk
    if head_major:
        out_specs = pl.BlockSpec((tn // 128, tm, 128), lambda i, j, k: (j, i, 0))
        out_shape = jax.ShapeDtypeStruct((N // 128, M, 128), out_dtype)
    else:
        out_specs = pl.BlockSpec((tm, tn), lambda i, j, k: (i, j))
        out_shape = jax.ShapeDtypeStruct((M, N), out_dtype)
    return pl.pallas_call(
        functools.partial(_mm_kernel, nk=nk, epilogue=epilogue, scale=scale, head_major=head_major),
        grid=(M // tm, N // tn, nk),
        in_specs=[pl.BlockSpec((tm, tk), lambda i, j, k: (i, k)),
                  pl.BlockSpec((tk, tn), lambda i, j, k: (k, j))],
        out_specs=out_specs, out_shape=out_shape,
        scratch_shapes=[pltpu.VMEM((tm, tn), F32)] if nk > 1 else [],
        compiler_params=_cparams(("parallel", "parallel", "arbitrary")),
        name=f"mm_{epilogue}",
    )(x, w)


def _mm_ln_kernel(x_ref, w_ref, res_ref, g_ref, b_ref, o32_ref, o16_ref, *, nk):
    k = pl.program_id(1)

    @pl.when(k == 0)
    def _():
        o32_ref[...] = jnp.dot(x_ref[...], w_ref[...], preferred_element_type=F32)

    @pl.when(k > 0)
    def _():
        o32_ref[...] += jnp.dot(x_ref[...], w_ref[...], preferred_element_type=F32)

    @pl.when(k == nk - 1)
    def _():
        def rows(i, carry):
            r = pl.ds(pl.multiple_of(i * LN_ROWS, LN_ROWS), LN_ROWS)
            y = _layer_norm_rows(DEEPNORM_ALPHA * res_ref[r, :] + o32_ref[r, :], g_ref[...], b_ref[...])
            o32_ref[r, :] = y
            o16_ref[r, :] = y.astype(BF16)
            return carry
        lax.fori_loop(0, o32_ref.shape[0] // LN_ROWS, rows, 0)


def matmul_ln(x, w, res, g, b, *, tm, tk):
    M, K = x.shape
    N = w.shape[1]
    tm, tk = min(tm, M), min(tk, K)
    assert M % tm == 0 and K % tk == 0 and tm % LN_ROWS == 0 and x.dtype == BF16
    nk = K // tk
    row = lambda i, k: (i, 0)
    vec = pl.BlockSpec((1, N), lambda i, k: (0, 0))
    return pl.pallas_call(
        functools.partial(_mm_ln_kernel, nk=nk),
        grid=(M // tm, nk),
        in_specs=[pl.BlockSpec((tm, tk), lambda i, k: (i, k)),
                  pl.BlockSpec((tk, N), lambda i, k: (k, 0)),
                  pl.BlockSpec((tm, N), row, pipeline_mode=pl.Buffered(1)),
                  vec, vec],
        out_specs=(pl.BlockSpec((tm, N), row), pl.BlockSpec((tm, N), row)),
        out_shape=(jax.ShapeDtypeStruct((M, N), F32), jax.ShapeDtypeStruct((M, N), BF16)),
        compiler_params=_cparams(("parallel", "arbitrary")),
        name="mm_ln",
    )(x, w, res, g.reshape(1, N).astype(F32), b.reshape(1, N).astype(F32))


def _mlstm_kernel(gb_ref, g_ref, q_ref, k_ref, v_ref, o_ref, ng_ref, c0_ref, n0_ref, m0_ref,
                  h_ref, c_ref, n_ref, m_ref, *, L, nvalid):
    hd = pl.program_id(1)
    c = pl.program_id(2)

    @pl.when(c == 0)
    def _():
        c_ref[...] = c0_ref[...]
        n_ref[...] = n0_ref[...]
        m_ref[...] = m0_ref[...]

    ig = g_ref[0, 0, 0, pl.ds(c, 1), :] + gb_ref[0, hd]
    fg = g_ref[0, 1, 0, pl.ds(c, 1), :] + gb_ref[1, hd]
    lf = -(jnp.maximum(-fg, 0.0) + jnp.log1p(jnp.exp(-jnp.abs(fg))))
    if nvalid < L:
        lane = lax.broadcasted_iota(jnp.int32, (1, L), 1)
        ig = jnp.where(lane < nvalid, ig, NEG_INF)
        lf = jnp.where(lane < nvalid, lf, 0.0)
    qi = lax.broadcasted_iota(jnp.int32, (L, L), 0)
    si = lax.broadcasted_iota(jnp.int32, (L, L), 1)
    eye = qi == si
    tri = si <= qi
    lf_col = jnp.sum(jnp.where(eye, lf, 0.0), axis=1, keepdims=True)
    b_col = jnp.sum(jnp.where(tri, lf, 0.0), axis=1, keepdims=True)
    b_row = jnp.sum(jnp.where(qi <= si, lf_col, 0.0), axis=0, keepdims=True)
    m_prev = m_ref[0, 0]
    dmat = jnp.where(tri, b_col - b_row + ig, NEG_INF)
    inter = b_col + m_prev
    mt = jnp.maximum(inter, jnp.max(dmat, axis=1, keepdims=True))
    w_intra = jnp.exp(dmat - mt)
    w_inter = jnp.exp(inter - mt)

    q = q_ref[...]
    kf = k_ref[...] * (M_DK ** -0.5)
    qb, kb, vb = q.astype(BF16), kf.astype(BF16), v_ref[...].astype(BF16)
    c_prev = c_ref[0, 0]
    n_prev = n_ref[0, 0]
    sc = lax.dot_general(qb, kb, (((1,), (1,)), ((), ())), preferred_element_type=F32) * w_intra
    num = (w_inter * jnp.dot(qb, c_prev.astype(BF16), preferred_element_type=F32)
           + jnp.dot(sc.astype(BF16), vb, preferred_element_type=F32))
    den = w_inter * jnp.sum(q * n_prev, axis=1, keepdims=True) + jnp.sum(sc, axis=1, keepdims=True)
    h = num / jnp.maximum(jnp.abs(den), jnp.exp(-mt))

    m_new = mt[L - 1:L, :]
    b_last = b_col[L - 1:L, :]
    w_end = jnp.exp(b_last - b_row + ig - m_new)
    w_end_col = jnp.sum(jnp.where(eye, w_end, 0.0), axis=1, keepdims=True)
    decay = jnp.exp(b_last + m_prev - m_new)
    kw = kf * w_end_col
    c_ref[0, 0] = decay * c_prev + lax.dot_general(
        kw.astype(BF16), vb, (((0,), (0,)), ((), ())), preferred_element_type=F32)
    n_ref[0, 0] = decay * n_prev + jnp.sum(kw, axis=0, keepdims=True)
    m_ref[0, 0] = m_new

    hn = h * lax.rsqrt(jnp.mean(h * h, axis=-1, keepdims=True) + LN_EPS) * ng_ref[...]
    h_ref[...] = (jax.nn.sigmoid(o_ref[...]) * hn).astype(h_ref.dtype)


def mlstm(proj, gates, gate_b, norm_g, state, layer, *, B, S, L, nvalid, out_cols):
    nc = S // L
    H, dk, dv = M_HEADS, M_DK, M_DV
    row = lambda b, h, c: b * nc + c
    if state is None:
        c0 = jnp.zeros((1, B, H, dk, dv), F32)
        n0 = jnp.zeros((1, B, H, 1, dk), F32)
        m0 = jnp.zeros((1, B, H, 1, 1), F32)
        layer = 0
    else:
        c0 = state[0]
        n0 = state[1].reshape(state[1].shape[0], B, H, 1, dk)
        m0 = state[2].reshape(state[2].shape[0], B, H, 1, 1)
    st_in = lambda b, h, c: (layer, b, h, 0, 0)
    st4 = lambda b, h, c: (b, h, 0, 0)
    return pl.pallas_call(
        functools.partial(_mlstm_kernel, L=L, nvalid=nvalid),
        grid=(B, H, nc),
        in_specs=[
            pl.BlockSpec(memory_space=pltpu.SMEM),
            pl.BlockSpec((1, 2, 1, nc, L), lambda b, h, c: (b, 0, h, 0, 0)),
            pl.BlockSpec((L, dk), lambda b, h, c: (row(b, h, c), h)),
            pl.BlockSpec((L, dk), lambda b, h, c: (row(b, h, c), H + h)),
            pl.BlockSpec((L, dv), lambda b, h, c: (row(b, h, c), H + h)),
            pl.BlockSpec((L, dv), lambda b, h, c: (row(b, h, c), 2 * H + h)),
            pl.BlockSpec((1, dv), lambda b, h, c: (0, h)),
            pl.BlockSpec((None, 1, 1, dk, dv), st_in),
            pl.BlockSpec((None, 1, 1, 1, dk), st_in),
            pl.BlockSpec((None, 1, 1, 1, 1), st_in),
        ],
        out_specs=(
            pl.BlockSpec((L, dv), lambda b, h, c: (row(b, h, c), h)),
            pl.BlockSpec((1, 1, dk, dv), st4),
            pl.BlockSpec((1, 1, 1, dk), st4),
            pl.BlockSpec((1, 1, 1, 1), st4),
        ),
        out_shape=(
            jax.ShapeDtypeStruct((B * S, out_cols), BF16),
            jax.ShapeDtypeStruct((B, H, dk, dv), F32),
            jax.ShapeDtypeStruct((B, H, 1, dk), F32),
            jax.ShapeDtypeStruct((B, H, 1, 1), F32),
        ),
        compiler_params=_cparams(("parallel", "parallel", "arbitrary")),
        name="mlstm",
    )(gate_b, gates, proj, proj, proj, proj, norm_g.reshape(1, H * dv), c0, n0, m0)


def _attn_prompt_kernel(*refs, dil, heads, head0, first, last, with_prev):
    bc_ref, bp_ref, q_ref, kc_ref, vc_ref = refs[:5]
    refs = refs[5:]
    if with_prev:
        kp_ref, vp_ref = refs[:2]
        refs = refs[2:]
    if not first:
        num_in_ref, sm_in_ref = refs[:2]
        refs = refs[2:]
    if last:
        _, out_ref, o32_ref = refs
    else:
        num_out_ref, sm_out_ref = refs
    has_prev = pl.program_id(2) > 0
    T = A_J
    lane = lax.broadcasted_iota(jnp.int32, (T, 128), 1)
    nt = (((1,), (1,)), ((), ()))
    for r in range(dil):
        rows = pl.ds(r, T, stride=dil) if dil > 1 else pl.ds(0, T)
        sm_old = None if first else sm_in_ref[rows, :]
        sm_new = jnp.zeros((T, 128), F32)
        for hl in range(heads):
            hg = head0 + hl
            qh = (q_ref[hl, rows, :] * (A_DH ** -0.5)).astype(BF16)
            lc = lax.dot_general(qh, kc_ref[hl, rows, :].astype(BF16), nt, preferred_element_type=F32) + bc_ref[hg]
            mx = jnp.max(lc, axis=1, keepdims=True)
            if with_prev:
                lp = lax.dot_general(qh, kp_ref[hl, rows, :].astype(BF16), nt, preferred_element_type=F32) + bp_ref[hg]
                lp = jnp.where(has_prev, lp, NEG_INF)
                mx = jnp.maximum(mx, jnp.max(lp, axis=1, keepdims=True))
            pc = jnp.exp(lc - mx)
            s = jnp.sum(pc, axis=1, keepdims=True)
            o = jnp.dot(pc.astype(BF16), vc_ref[hl, rows, :].astype(BF16), preferred_element_type=F32)
            if with_prev:
                pp = jnp.exp(lp - mx)
                s = s + jnp.sum(pp, axis=1, keepdims=True)
                o = o + jnp.dot(pp.astype(BF16), vp_ref[hl, rows, :].astype(BF16), preferred_element_type=F32)
            if not first:
                s_old = sm_old[:, hg:hg + 1]
                m_old = sm_old[:, A_HEADS + hg:A_HEADS + hg + 1]
                m_all = jnp.maximum(m_old, mx)
                a_old = jnp.exp(m_old - m_all)
                a_new = jnp.exp(mx - m_all)
                o = a_old * num_in_ref[hl, rows, :] + a_new * o
                s = a_old * s_old + a_new * s
                mx = m_all
            if last:
                o32_ref[hl, rows, :] = o / s
            else:
                num_out_ref[hl, rows, :] = o
                sm_new = jnp.where(lane == hg, s, sm_new)
                sm_new = jnp.where(lane == A_HEADS + hg, mx, sm_new)
        if not last:
            sm_out_ref[rows, :] = sm_new
    if last:
        for hl in range(heads):
            out_ref[:, hl * A_DH:(hl + 1) * A_DH] = o32_ref[hl].astype(out_ref.dtype)


def attn_prompt_branch(att, bias_c, bias_p, state, mix, *, B, S, g, dil, hb):
    first, last = state is None, mix is not None
    span = A_J * dil
    nb = S // span
    with_prev = nb > 1
    nhb = A_HEADS // hb
    assert nhb == 1 or last
    cur = lambda part: pl.BlockSpec((hb, span, A_DH), lambda b, c, n: ((3 * g + part) * nhb + c, b * nb + n, 0))
    prev = lambda part: pl.BlockSpec(
        (hb, span, A_DH), lambda b, c, n: ((3 * g + part) * nhb + c, jnp.maximum(b * nb + n - 1, 0), 0))
    bias_spec = pl.BlockSpec((A_HEADS, A_J, A_J), lambda b, c, n: (0, 0, 0))
    num_spec = pl.BlockSpec((hb, span, A_DH), lambda b, c, n: (c, b * nb + n, 0))
    sm_spec = pl.BlockSpec((span, 128), lambda b, c, n: (b * nb + n, 0))
    in_specs = [bias_spec, bias_spec, cur(0), cur(1), cur(2)]
    args = [bias_c, bias_p, att, att, att]
    if with_prev:
        in_specs += [prev(1), prev(2)]
        args += [att, att]
    if not first:
        in_specs += [num_spec, sm_spec]
        args += list(state)
    aliases = {}
    if last:
        in_specs.append(pl.BlockSpec(memory_space=pl.ANY))
        args.append(mix)
        aliases = {len(args) - 1: 0}
        out_specs = pl.BlockSpec((span, hb * A_DH), lambda b, c, n: (b * nb + n, MIX_ATT_COL * nhb + c))
        out_shape = jax.ShapeDtypeStruct(mix.shape, mix.dtype)
    else:
        out_specs = (num_spec, sm_spec)
        out_shape = (jax.ShapeDtypeStruct((A_HEADS, B * S, A_DH), F32), jax.ShapeDtypeStruct((B * S, 128), F32))

    def body(*refs):
        kw = dict(dil=dil, heads=hb, first=first, last=last, with_prev=with_prev)
        if nhb == 1:
            _attn_prompt_kernel(*refs, head0=0, **kw)
        else:
            for cb in range(nhb):
                @pl.when(pl.program_id(1) == cb)
                def _(cb=cb):
                    _attn_prompt_kernel(*refs, head0=cb * hb, **kw)

    return pl.pallas_call(
        body, grid=(B, nhb, nb), in_specs=in_specs, out_specs=out_specs, out_shape=out_shape,
        input_output_aliases=aliases,
        scratch_shapes=[pltpu.VMEM((hb, span, A_DH), F32)] if last else [],
        compiler_params=_cparams(("parallel", "parallel", "arbitrary")),
        name=f"attn_prompt_{g}",
    )(*args)


def _attn_decode_kernel(bias_ref, a_ref, k1_ref, v1_ref, k2_ref, v2_ref, k3_ref, v3_ref, out_ref):
    caches = ((k1_ref, v1_ref), (k2_ref, v2_ref), (k3_ref, v3_ref))
    nt = (((1,), (1,)), ((), ()))
    hw = A_HEADS * A_DH
    for h in range(A_HEADS):
        outs, dens, maxs = [], [], []
        for g in range(3):
            base = 3 * g * hw + h * A_DH
            qf = (a_ref[0, :, base:base + A_DH] * (A_DH ** -0.5)).astype(BF16)
            kn = a_ref[0, :, base + hw:base + hw + A_DH].astype(BF16).astype(F32)
            vn = a_ref[0, :, base + 2 * hw:base + 2 * hw + A_DH].astype(BF16).astype(F32)
            kc = caches[g][0][:, h, :].astype(BF16)
            vc = caches[g][1][:, h, :].astype(BF16)
            q16 = jnp.broadcast_to(qf, (16, A_DH))
            lc = lax.dot_general(q16, kc, nt, preferred_element_type=F32)[0:1, :] + bias_ref[g, h:h + 1, 0:A_J]
            ln = (jnp.sum(qf.astype(F32) * kn, axis=1, keepdims=True)
                  + bias_ref[g, h:h + 1, A_J:A_J + 1])
            mx = jnp.maximum(jnp.max(lc, axis=1, keepdims=True), ln)
            pc = jnp.exp(lc - mx)
            pn = jnp.exp(ln - mx)
            p16 = jnp.broadcast_to(pc.astype(BF16), (16, A_J))
            o = jnp.dot(p16, vc, preferred_element_type=F32)[0:1, :] + pn.astype(BF16).astype(F32) * vn
            outs.append(o)
            dens.append(jnp.sum(pc, axis=1, keepdims=True) + pn)
            maxs.append(mx)
        m_all = jnp.maximum(jnp.maximum(maxs[0], maxs[1]), maxs[2])
        wts = [jnp.exp(m - m_all) for m in maxs]
        num = wts[0] * outs[0] + wts[1] * outs[1] + wts[2] * outs[2]
        den = wts[0] * dens[0] + wts[1] * dens[1] + wts[2] * dens[2]
        out_ref[0, :, h * A_DH:(h + 1) * A_DH] = num / den


def attn_decode(att, caches, layer, bias_dec):
    B = att.shape[0]
    hw = A_HEADS * A_DH
    in_specs = [pl.BlockSpec(bias_dec.shape, lambda b: (0, 0, 0)),
                pl.BlockSpec((1, 1, N_ATT), lambda b: (b, 0, 0))]
    args = [bias_dec, att.reshape(B, 1, N_ATT)]
    for g, (window, dil) in enumerate(A_BRANCHES):
        for t in caches[2 * g:2 * g + 2]:
            assert t.shape[2] == window
            in_specs.append(pl.BlockSpec((None, None, A_J, None, A_HEADS, A_DH), lambda b: (layer, b, 0, 0, 0, 0)))
            args.append(t.reshape(t.shape[0], B, window // dil, dil, A_HEADS, A_DH))
    out = pl.pallas_call(
        _attn_decode_kernel, grid=(B,), in_specs=in_specs,
        out_specs=pl.BlockSpec((1, 1, hw), lambda b: (b, 0, 0)),
        out_shape=jax.ShapeDtypeStruct((B, 1, hw), F32),
        compiler_params=_cparams(("parallel",)),
        name="attn_decode",
    )(*args)
    return out.reshape(B, hw)


def _shift_kernel(c_ref, new_ref, o_ref):
    P = c_ref.shape[0]
    R = min(COPY_ROWS, P)

    def chunk(i, carry):
        o_ref[pl.ds(i * R, R)] = c_ref[pl.ds(i * R + 1, R)]
        return carry
    lax.fori_loop(0, P // R - 1, chunk, 0)
    o_ref[P - R:P - 1] = c_ref[P - R + 1:P]
    o_ref[P - 1:P] = new_ref[...]


def shift_append(cache, new_rows):
    Dp, B, P, H, Dh = cache.shape
    blk = lambda rows: pl.BlockSpec((None, None, rows, H, Dh), lambda l, b: (l, b, 0, 0, 0))
    return pl.pallas_call(
        _shift_kernel, grid=(Dp, B), in_specs=[blk(P), blk(1)], out_specs=blk(P),
        out_shape=jax.ShapeDtypeStruct(cache.shape, cache.dtype),
        compiler_params=_cparams(("parallel", "parallel")),
        name="shift_append",
    )(cache, new_rows.astype(cache.dtype))


def _conv_kernel(cv_ref, cg_ref, hv_ref, hg_ref, past_ref, cw_ref, cb_ref, lg_ref, lb_ref, *rest,
                 TS, RC, nvalid):
    y_ref, tail_ref, xs_ref = rest[-3:]
    t = pl.program_id(1)
    H = CONV_HALO

    @pl.when(t == 0)
    def _():
        xs_ref[0:H, :] = past_ref[0]

    @pl.when(t > 0)
    def _():
        xs_ref[0:H, :] = hv_ref[...] * jax.nn.sigmoid(hg_ref[...])

    xs_ref[H:H + TS, :] = cv_ref[...] * jax.nn.sigmoid(cg_ref[...])
    lead = H - (C_WIDTH - 1)
    for rc in range(TS // RC):
        acc = jnp.zeros((RC, C_CH), F32)
        for w in range(C_WIDTH):
            r0 = rc * RC + lead + w
            acc = acc + xs_ref[r0:r0 + RC, :] * cw_ref[w:w + 1, :]
        y = _layer_norm_rows(acc + cb_ref[...], lg_ref[...], lb_ref[...])
        y_ref[rc * RC:(rc + 1) * RC, :] = (y * jax.nn.sigmoid(y)).astype(y_ref.dtype)
    tail_ref[0] = xs_ref[nvalid:nvalid + H, :]


def conv_block(proj, past, conv_w, conv_b, ln_g, ln_b, mix, *, B, S, TS, nvalid, vcol):
    nt = S // TS
    H = CONV_HALO
    RC = min(TS, 32)
    past_p = jnp.pad(past.astype(F32), ((0, 0), (H - (C_WIDTH - 1), 0), (0, 0)))
    cur = lambda col: pl.BlockSpec((TS, C_CH), lambda b, t: (b * nt + t, col))
    halo_rows = min(H, B * S)
    halo = lambda col: pl.BlockSpec(
        (halo_rows, C_CH), lambda b, t: (jnp.maximum((b * S + t * TS) // H - 1, 0), col))
    vec = pl.BlockSpec((1, C_CH), lambda b, t: (0, 0))
    in_specs = [cur(vcol), cur(vcol + 1), halo(vcol), halo(vcol + 1),
                pl.BlockSpec((1, H, C_CH), lambda b, t: (b, 0, 0)),
                pl.BlockSpec((H, C_CH), lambda b, t: (0, 0)), vec, vec, vec]
    args = [proj, proj, proj, proj, past_p, jnp.pad(conv_w.astype(F32), ((0, H - C_WIDTH), (0, 0))),
            conv_b.reshape(1, C_CH), ln_g.reshape(1, C_CH), ln_b.reshape(1, C_CH)]
    aliases = {}
    if mix is None:
        y_spec = pl.BlockSpec((TS, C_CH), lambda b, t: (b * nt + t, 0))
        y_shape = jax.ShapeDtypeStruct((B * S, C_CH), BF16)
    else:
        in_specs.append(pl.BlockSpec(memory_space=pl.ANY))
        args.append(mix)
        aliases = {len(args) - 1: 0}
        y_spec = pl.BlockSpec((TS, C_CH), lambda b, t: (b * nt + t, MIX_CONV_COL))
        y_shape = jax.ShapeDtypeStruct(mix.shape, mix.dtype)
    y, tail = pl.pallas_call(
        functools.partial(_conv_kernel, TS=TS, RC=RC, nvalid=nvalid),
        grid=(B, nt), in_specs=in_specs,
        out_specs=(y_spec, pl.BlockSpec((1, H, C_CH), lambda b, t: (b, 0, 0))),
        out_shape=(y_shape, jax.ShapeDtypeStruct((B, H, C_CH), F32)),
        scratch_shapes=[pltpu.VMEM((H + TS, C_CH), F32)],
        input_output_aliases=aliases,
        compiler_params=_cparams(("parallel", "arbitrary")),
        name="conv_ln_silu",
    )(*args)
    return y, tail[:, H - (C_WIDTH - 1):]


def _xattn_kernel(q_ref, mk_ref, mv_ref, o_ref):
    q = q_ref[0]
    logits = lax.dot_general(q, mk_ref[...].astype(BF16), (((1,), (1,)), ((), ())),
                             preferred_element_type=F32)
    e = jnp.exp(logits - jnp.max(logits, axis=-1, keepdims=True))
    p = e / jnp.sum(e, axis=-1, keepdims=True)
    o_ref[0] = jnp.dot(p.astype(BF16), mv_ref[...].astype(BF16), preferred_element_type=F32).astype(o_ref.dtype)


def cross_attn(q, mk, mv, layer, *, TS):
    B, S, _ = q.shape
    TS = min(TS, S)
    mem = pl.BlockSpec((None, None, N_MEM, XA_DH), lambda b, h, t: (layer, b, 0, h))
    return pl.pallas_call(
        _xattn_kernel, grid=(B, XA_HEADS, S // TS),
        in_specs=[pl.BlockSpec((1, TS, XA_DH), lambda b, h, t: (b, t, h)), mem, mem],
        out_specs=pl.BlockSpec((1, TS, XA_DH), lambda b, h, t: (b, t, h)),
        out_shape=jax.ShapeDtypeStruct(q.shape, BF16),
        compiler_params=_cparams(("parallel", "parallel", "arbitrary")),
        name="cross_attn",
    )(q, mk, mv)


def _rel_bucket(dist):
    exact = N_BUCKETS // 2
    lg = jnp.log(jnp.maximum(dist, 1).astype(F32) / exact) / math.log(REL_MAX_DIST / exact)
    large = jnp.minimum(exact + (lg * (N_BUCKETS - exact)).astype(jnp.int32), N_BUCKETS - 1)
    return jnp.where(dist < exact, dist, large)


def _bias_tables(rel_bias):
    J = A_J
    qi = jnp.arange(J)[:, None]
    ki = jnp.arange(J)[None, :]
    cur, prev, dec = [], [], []
    for g, (window, dil) in enumerate(A_BRANCHES):
        table = rel_bias[:, g * A_HEADS:(g + 1) * A_HEADS].astype(F32)
        bias_j = table[_rel_bucket(dil * jnp.arange(window // dil + 1))]
        bt = bias_j.T
        rel_c = qi - ki
        rel_p = qi + J - ki
        cur.append(jnp.where((rel_c >= 0)[None], bt[:, jnp.clip(rel_c, 0, J)], NEG_INF))
        prev.append(jnp.where((rel_p <= J)[None], bt[:, jnp.clip(rel_p, 0, J)], NEG_INF))
        row = jnp.concatenate([bt[:, J - jnp.arange(J)], bt[:, 0:1]], axis=1)
        dec.append(jnp.pad(row, ((0, 0), (0, 2 * J - (J + 1)))))
    return cur, prev, jnp.stack(dec)


def _prep_layer_weights(l, w_in, w_out, xa_wq, xa_wk, xa_wv, xa_wo, ffn_w1, ffn_w2):
    wi = w_in[l]
    w_gate = jnp.pad(wi[:, N_MAIN:N_MAIN + N_GATE], ((0, 0), (0, GATE_PAD - N_GATE))).astype(BF16)
    cast = lambda w: w[l].astype(BF16)
    w_main = jnp.concatenate([wi[:, :N_MAIN], wi[:, N_MAIN + N_GATE + N_ATT:]], axis=1).astype(BF16)
    w_att = wi[:, N_MAIN + N_GATE:N_MAIN + N_GATE + N_ATT].astype(BF16)
    return dict(main=w_main, att=w_att, gate=w_gate,
                out=cast(w_out), wq=cast(xa_wq), wk=cast(xa_wk), wv=cast(xa_wv),
                wo=cast(xa_wo), w1=cast(ffn_w1), w2=cast(ffn_w2))


def _gate_layout(gmat, B, S, L):
    g = gmat[:, :N_GATE].reshape(B, S // L, L, 2, M_HEADS)
    return g.transpose(0, 3, 4, 1, 2)


def _trunk_tail(x32, x16, mix16, mk, mv, layer, w, lg, lb, *, B, S, tm, tmm, xa_ts, pad_q):
    x32, x16 = matmul_ln(mix16, w["out"], x32, lg[0], lb[0], tm=tm, tk=512)
    q = matmul(x16, w["wq"], tm=tmm, tn=1024, tk=D_MODEL, out_dtype=BF16, scale=XA_DH ** -0.5)
    if pad_q > 1:
        qs = jnp.zeros((B, pad_q, D_MODEL), BF16).at[:, 0].set(q[:B])
        o = cross_attn(qs, mk, mv, layer, TS=pad_q)[:, 0]
        o = jnp.pad(o, ((0, q.shape[0] - B), (0, 0)))
    else:
        o = cross_attn(q.reshape(B, S, D_MODEL), mk, mv, layer, TS=xa_ts).reshape(B * S, D_MODEL)
    x32, x16 = matmul_ln(o, w["wo"], x32, lg[1], lb[1], tm=tm, tk=512)
    hid = matmul(x16, w["w1"], tm=tmm, tn=1024, tk=D_MODEL, out_dtype=BF16, epilogue="relu2")
    x32, x16 = matmul_ln(hid, w["w2"], x32, lg[2], lb[2], tm=tm, tk=512)
    return x32, x16


def kernel(x_prompt, x_sample, mem_prompt, cache_win_k1, cache_win_v1, cache_win_k2, cache_win_v2, cache_win_k3, cache_win_v3, state_mlstm_C, state_mlstm_n, state_mlstm_m, state_conv, cache_mem_k, cache_mem_v, rel_bias, w_in, mlstm_gate_bias, mlstm_norm_g, conv_w, conv_b, conv_ln_g, conv_ln_b, w_out, xa_wq, xa_wk, xa_wv, xa_wo, ffn_w1, ffn_w2, ln_g, ln_b):
    B, S, D = x_prompt.shape
    BS = x_sample.shape[0]
    MS = SAMPLE_ROWS
    L = math.gcd(S, M_CHUNK)
    bias_c, bias_p, bias_dec = _bias_tables(rel_bias)
    cache_win = (cache_win_k1, cache_win_v1, cache_win_k2, cache_win_v2, cache_win_k3, cache_win_v3)
    sample_state = (state_mlstm_C.astype(F32), state_mlstm_n.astype(F32), state_mlstm_m.astype(F32))
    smem_k = cache_mem_k.reshape(DEPTH, BS, N_MEM, D)
    smem_v = cache_mem_v.reshape(DEPTH, BS, N_MEM, D)

    xp32 = x_prompt.reshape(B * S, D)
    xp16 = xp32.astype(BF16)
    xs32 = jnp.pad(x_sample.reshape(BS, D), ((0, MS - BS), (0, 0)))
    xs16 = xs32.astype(BF16)
    mem16 = mem_prompt.reshape(B * N_MEM, D).astype(BF16)

    p_win = [[] for _ in range(6)]
    new_rows = [[] for _ in range(6)]
    p_C, p_n, p_m, p_conv, p_mk, p_mv = [], [], [], [], [], []
    s_C, s_n, s_m, s_conv = [], [], [], []
    for l in range(DEPTH):
        w = _prep_layer_weights(l, w_in, w_out, xa_wq, xa_wk, xa_wv, xa_wo, ffn_w1, ffn_w2)
        gate_b = mlstm_gate_bias[l].astype(F32)

        main = matmul(xp16, w["main"], tm=1024, tn=1024, tk=D)
        att = matmul(xp16, w["att"], tm=1024, tn=1024, tk=D, head_major=True)
        gmat = matmul(xp16, w["gate"], tm=1024, tn=GATE_PAD, tk=D)
        mix, C1, n1, m1 = mlstm(main, _gate_layout(gmat, B, S, L), gate_b, mlstm_norm_g[l], None, 0,
                                B=B, S=S, L=L, nvalid=L, out_cols=D)
        state = None
        for g, (window, dil) in enumerate(A_BRANCHES):
            last = g == len(A_BRANCHES) - 1
            out = attn_prompt_branch(att, bias_c[g], bias_p[g], state, mix if last else None,
                                     B=B, S=S, g=g, dil=dil, hb=A_HEADS // 2 if last else A_HEADS)
            if last:
                mix = out
            else:
                state = out
            keep = min(window, S)
            for j in range(2):
                h0 = (3 * g + 1 + j) * A_HEADS
                kv = att[h0:h0 + A_HEADS].reshape(A_HEADS, B, S, A_DH)[:, :, S - keep:]
                p_win[2 * g + j].append(kv.transpose(1, 2, 0, 3))
        mix, conv_tail = conv_block(main, jnp.zeros((B, C_WIDTH - 1, C_CH), F32), conv_w[l], conv_b[l],
                                    conv_ln_g[l], conv_ln_b[l], mix, B=B, S=S, TS=128, nvalid=128, vcol=CONV_COL)
        mk = matmul(mem16, w["wk"], tm=1024, tn=1024, tk=D)
        mv = matmul(mem16, w["wv"], tm=1024, tn=1024, tk=D)
        xp32, xp16 = _trunk_tail(xp32, xp16, mix, mk.reshape(1, B, N_MEM, D), mv.reshape(1, B, N_MEM, D), 0, w,
                                 ln_g[l], ln_b[l], B=B, S=S, tm=512, tmm=1024, xa_ts=512, pad_q=1)
        p_C.append(C1); p_n.append(n1.reshape(B, M_HEADS, M_DK)); p_m.append(m1.reshape(B, M_HEADS))
        p_conv.append(conv_tail)
        p_mk.append(mk.reshape(B, N_MEM, XA_HEADS, XA_DH)); p_mv.append(mv.reshape(B, N_MEM, XA_HEADS, XA_DH))

        smain = matmul(xs16, w["main"], tm=MS, tn=1024, tk=D)
        satt = matmul(xs16, w["att"], tm=MS, tn=1024, tk=D)[:BS]
        sgm = matmul(xs16, w["gate"], tm=MS, tn=GATE_PAD, tk=D)
        LS = M_CHUNK
        main_pad = jnp.zeros((BS, LS, N_MAIN), F32).at[:, 0].set(smain[:BS, :N_MAIN]).reshape(BS * LS, N_MAIN)
        gate_pad = jnp.zeros((BS, LS, GATE_PAD), F32).at[:, 0].set(sgm[:BS]).reshape(BS * LS, GATE_PAD)
        shm, Cs, ns, ms = mlstm(main_pad, _gate_layout(gate_pad, BS, LS, LS), gate_b, mlstm_norm_g[l],
                                sample_state, l, B=BS, S=LS, L=LS, nvalid=1, out_cols=M_HEADS * M_DV)
        shm = shm.reshape(BS, LS, M_HEADS * M_DV)[:, 0]
        sha = attn_decode(satt, cache_win, l, bias_dec)
        for g in range(len(A_BRANCHES)):
            for j in range(2):
                c0 = (3 * g + 1 + j) * COL_BLK
                new_rows[2 * g + j].append(satt[:, c0:c0 + COL_BLK].reshape(BS, 1, A_HEADS, A_DH))
        CT = 8
        cpad = jnp.zeros((BS, CT, 2 * C_CH), F32).at[:, 0].set(smain[:BS, N_MAIN:]).reshape(BS * CT, 2 * C_CH)
        syc, sconv_tail = conv_block(cpad, state_conv[l], conv_w[l], conv_b[l], conv_ln_g[l], conv_ln_b[l], None,
                                     B=BS, S=CT, TS=CT, nvalid=1, vcol=0)
        syc = syc.reshape(BS, CT, C_CH)[:, 0]
        smix = jnp.concatenate([shm, sha.astype(BF16), syc], axis=1)
        smix = jnp.pad(smix, ((0, MS - BS), (0, 0)))
        xs32, xs16 = _trunk_tail(xs32, xs16, smix, smem_k, smem_v, l, w, ln_g[l], ln_b[l],
                                 B=BS, S=1, tm=MS, tmm=MS, xa_ts=MS, pad_q=MS)
        s_C.append(Cs); s_n.append(ns.reshape(BS, M_HEADS, M_DK)); s_m.append(ms.reshape(BS, M_HEADS))
        s_conv.append(sconv_tail)

    st = lambda xs: jnp.stack(xs)
    s_win = [shift_append(cache_win[i], st(new_rows[i])) for i in range(6)]
    return (xp32.reshape(B, S, D), xs32[:BS].reshape(BS, 1, D),
            st(p_win[0]), st(p_win[1]), st(p_win[2]), st(p_win[3]), st(p_win[4]), st(p_win[5]),
            st(p_C), st(p_n), st(p_m), st(p_conv), st(p_mk), st(p_mv),
            s_win[0], s_win[1], s_win[2], s_win[3], s_win[4], s_win[5],
            st(s_C), st(s_n), st(s_m), st(s_conv))
```

```python
import functools
import math

import jax
import jax.numpy as jnp
from jax import lax
from jax.experimental import pallas as pl
from jax.experimental.pallas import tpu as pltpu

F32 = jnp.float32
BF16 = jnp.bfloat16

D_MODEL = 4096
DEPTH = 2
M_HEADS, M_DK, M_DV, M_CHUNK = 4, 256, 512, 64
A_HEADS, A_DH = 8, 128
A_BRANCHES = ((128, 1), (512, 4), (2048, 16))
A_J = 128
C_CH, C_WIDTH = 1024, 31
N_BUCKETS, REL_MAX_DIST = 32, 2048
XA_HEADS = 4
XA_DH = D_MODEL // XA_HEADS
N_MEM = 256
D_FF = 4 * D_MODEL
DEEPNORM_ALPHA = (2 * DEPTH) ** 0.25
LN_EPS = 1e-5

N_MAIN = 2 * M_HEADS * M_DK + 2 * M_HEADS * M_DV
N_GATE = 2 * M_HEADS
N_ATT = 9 * A_HEADS * A_DH
COL_BLK = 1024
CONV_COL = N_MAIN // COL_BLK
MIX_ATT_COL = M_HEADS * M_DV // COL_BLK
MIX_CONV_COL = MIX_ATT_COL + 1
GATE_PAD = 128
CONV_HALO = 32
LN_ROWS = 16
SAMPLE_ROWS = 16
COPY_ROWS = 64

VMEM_LIMIT = 56 * 1024 * 1024
NEG_INF = float("-inf")


def _cparams(sem):
    return pltpu.CompilerParams(dimension_semantics=sem, vmem_limit_bytes=VMEM_LIMIT)


def _layer_norm_rows(z, g, b):
    mu = jnp.mean(z, axis=-1, keepdims=True)
    zc = z - mu
    var = jnp.mean(zc * zc, axis=-1, keepdims=True)
    return zc * lax.rsqrt(var + LN_EPS) * g + b


def _mm_kernel(x_ref, w_ref, o_ref, *scratch, nk, epilogue, scale, head_major):
    def finish(acc):
        if epilogue == "relu2":
            r = jnp.maximum(acc, 0.0)
            acc = r * r
        elif scale != 1.0:
            acc = acc * scale
        if head_major:
            for j in range(o_ref.shape[0]):
                o_ref[j] = acc[:, j * 128:(j + 1) * 128].astype(o_ref.dtype)
        else:
            o_ref[...] = acc.astype(o_ref.dtype)

    def prod():
        return jnp.dot(x_ref[...].astype(BF16), w_ref[...], preferred_element_type=F32)

    if nk == 1:
        finish(prod())
        return
    (acc_ref,) = scratch
    k = pl.program_id(2)

    @pl.when(k == 0)
    def _():
        acc_ref[...] = prod()

    @pl.when(k > 0)
    def _():
        acc_ref[...] += prod()

    @pl.when(k == nk - 1)
    def _():
        finish(acc_ref[...])


def matmul(x, w, layer, *, tm, tn, tk, out_dtype=F32, epilogue="none", scale=1.0, head_major=False):
    M, K = x.shape
    N = w.shape[2]
    tm, tn, tk = min(tm, M), min(tn, N), min(tk, K)
    assert M % tm == 0 and N % tn == 0 and K % tk == 0, (x.shape, w.shape, tm, tn, tk)
    nk = K // tk
    if head_major:
        out_specs = pl.BlockSpec((tn // 128, tm, 128), lambda i, j, k: (j, i, 0))
        out_shape = jax.ShapeDtypeStruct((N // 128, M, 128), out_dtype)
    else:
        out_specs = pl.BlockSpec((tm, tn), lambda i, j, k: (i, j))
        out_shape = jax.ShapeDtypeStruct((M, N), out_dtype)
    return pl.pallas_call(
        functools.partial(_mm_kernel, nk=nk, epilogue=epilogue, scale=scale, head_major=head_major),
        grid=(M // tm, N // tn, nk),
        in_specs=[pl.BlockSpec((tm, tk), lambda i, j, k: (i, k)),
                  pl.BlockSpec((None, tk, tn), lambda i, j, k: (layer, k, j))],
        out_specs=out_specs, out_shape=out_shape,
        scratch_shapes=[pltpu.VMEM((tm, tn), F32)] if nk > 1 else [],
        compiler_params=_cparams(("parallel", "parallel", "arbitrary")),
        name=f"mm_{epilogue}",
    )(x, w)


def _post_ln_kernel(res_ref, sub_ref, g_ref, b_ref, o32_ref, o16_ref):
    def rows(i, carry):
        r = pl.ds(pl.multiple_of(i * LN_ROWS, LN_ROWS), LN_ROWS)
        y = _layer_norm_rows(DEEPNORM_ALPHA * res_ref[r, :] + sub_ref[r, :], g_ref[...], b_ref[...])
        o32_ref[r, :] = y
        o16_ref[r, :] = y.astype(BF16)
        return carry
    lax.fori_loop(0, o32_ref.shape[0] // LN_ROWS, rows, 0, unroll=2)


def post_ln(res, sub, g, b, *, tr):
    M, N = res.shape
    tr = min(tr, M)
    assert M % tr == 0 and tr % (2 * LN_ROWS) == 0 or tr == LN_ROWS
    row = pl.BlockSpec((tr, N), lambda i: (i, 0))
    vec = pl.BlockSpec((1, N), lambda i: (0, 0))
    return pl.pallas_call(
        _post_ln_kernel, grid=(M // tr,), in_specs=[row, row, vec, vec], out_specs=(row, row),
        out_shape=(jax.ShapeDtypeStruct((M, N), F32), jax.ShapeDtypeStruct((M, N), BF16)),
        compiler_params=_cparams(("parallel",)),
        name="post_ln",
    )(res, sub, g.reshape(1, N).astype(F32), b.reshape(1, N).astype(F32))


def _mlstm_kernel(gb_ref, g_ref, q_ref, k_ref, v_ref, o_ref, ng_ref, c0_ref, n0_ref, m0_ref,
                  h_ref, c_ref, n_ref, m_ref, *, L, nvalid):
    c = pl.program_id(1)

    @pl.when(c == 0)
    def _():
        c_ref[...] = c0_ref[...]
        n_ref[...] = n0_ref[...]
        m_ref[...] = m0_ref[...]

    for hd in range(M_HEADS):
        _mlstm_head(hd, c, gb_ref, g_ref, q_ref, k_ref, v_ref, o_ref, ng_ref, h_ref, c_ref, n_ref, m_ref,
                    L=L, nvalid=nvalid)


def _mlstm_head(hd, c, gb_ref, g_ref, q_ref, k_ref, v_ref, o_ref, ng_ref, h_ref, c_ref, n_ref, m_ref, *, L, nvalid):
    dk, dv = M_DK, M_DV
    ks, vs = slice(hd * dk, (hd + 1) * dk), slice(hd * dv, (hd + 1) * dv)
    ig = g_ref[0, 0, hd, pl.ds(c, 1), :] + gb_ref[0, hd]
    fg = g_ref[0, 1, hd, pl.ds(c, 1), :] + gb_ref[1, hd]
    lf = -(jnp.maximum(-fg, 0.0) + jnp.log1p(jnp.exp(-jnp.abs(fg))))
    if nvalid < L:
        lane = lax.broadcasted_iota(jnp.int32, (1, L), 1)
        ig = jnp.where(lane < nvalid, ig, NEG_INF)
        lf = jnp.where(lane < nvalid, lf, 0.0)
    qi = lax.broadcasted_iota(jnp.int32, (L, L), 0)
    si = lax.broadcasted_iota(jnp.int32, (L, L), 1)
    eye = qi == si
    tri = si <= qi
    lf_col = jnp.sum(jnp.where(eye, lf, 0.0), axis=1, keepdims=True)
    b_col = jnp.sum(jnp.where(tri, lf, 0.0), axis=1, keepdims=True)
    b_row = jnp.sum(jnp.where(qi <= si, lf_col, 0.0), axis=0, keepdims=True)
    m_prev = m_ref[0, hd]
    dmat = jnp.where(tri, b_col - b_row + ig, NEG_INF)
    inter = b_col + m_prev
    mt = jnp.maximum(inter, jnp.max(dmat, axis=1, keepdims=True))
    w_intra = jnp.exp(dmat - mt)
    w_inter = jnp.exp(inter - mt)

    q = q_ref[:, ks]
    kf = k_ref[:, ks] * (M_DK ** -0.5)
    qb, kb, vb = q.astype(BF16), kf.astype(BF16), v_ref[:, vs].astype(BF16)
    c_prev = c_ref[0, hd]
    n_prev = n_ref[0, hd]
    sc = lax.dot_general(qb, kb, (((1,), (1,)), ((), ())), preferred_element_type=F32) * w_intra
    num = (w_inter * jnp.dot(qb, c_prev.astype(BF16), preferred_element_type=F32)
           + jnp.dot(sc.astype(BF16), vb, preferred_element_type=F32))
    den = w_inter * jnp.sum(q * n_prev, axis=1, keepdims=True) + jnp.sum(sc, axis=1, keepdims=True)
    h = num / jnp.maximum(jnp.abs(den), jnp.exp(-mt))

    m_new = mt[L - 1:L, :]
    b_last = b_col[L - 1:L, :]
    w_end = jnp.exp(b_last - b_row + ig - m_new)
    w_end_col = jnp.sum(jnp.where(eye, w_end, 0.0), axis=1, keepdims=True)
    decay = jnp.exp(b_last + m_prev - m_new)
    kw = kf * w_end_col
    c_ref[0, hd] = decay * c_prev + lax.dot_general(
        kw.astype(BF16), vb, (((0,), (0,)), ((), ())), preferred_element_type=F32)
    n_ref[0, hd] = decay * n_prev + jnp.sum(kw, axis=0, keepdims=True)
    m_ref[0, hd] = m_new

    hn = h * lax.rsqrt(jnp.mean(h * h, axis=-1, keepdims=True) + LN_EPS) * ng_ref[:, vs]
    h_ref[:, vs] = (jax.nn.sigmoid(o_ref[:, vs]) * hn).astype(h_ref.dtype)


def mlstm(proj, gates, gate_b, norm_g, state, layer, *, B, S, L, nvalid, out_cols):
    nc = S // L
    H, dk, dv = M_HEADS, M_DK, M_DV
    if state is None:
        c0 = jnp.zeros((1, B, H, dk, dv), F32)
        n0 = jnp.zeros((1, B, H, 1, dk), F32)
        m0 = jnp.zeros((1, B, H, 1, 1), F32)
        layer = 0
    else:
        c0 = state[0]
        n0 = state[1].reshape(state[1].shape[0], B, H, 1, dk)
        m0 = state[2].reshape(state[2].shape[0], B, H, 1, 1)
    st_in = lambda b, c: (layer, b, 0, 0, 0)
    st4 = lambda b, c: (b, 0, 0, 0)
    row = lambda b, c: b * nc + c
    return pl.pallas_call(
        functools.partial(_mlstm_kernel, L=L, nvalid=nvalid),
        grid=(B, nc),
        in_specs=[
            pl.BlockSpec(memory_space=pltpu.SMEM),
            pl.BlockSpec((1, 2, H, nc, L), lambda b, c: (b, 0, 0, 0, 0)),
            pl.BlockSpec((L, H * dk), lambda b, c: (row(b, c), 0)),
            pl.BlockSpec((L, H * dk), lambda b, c: (row(b, c), 1)),
            pl.BlockSpec((L, H * dv), lambda b, c: (row(b, c), 1)),
            pl.BlockSpec((L, H * dv), lambda b, c: (row(b, c), 2)),
            pl.BlockSpec((1, H * dv), lambda b, c: (0, 0)),
            pl.BlockSpec((None, 1, H, dk, dv), st_in),
            pl.BlockSpec((None, 1, H, 1, dk), st_in),
            pl.BlockSpec((None, 1, H, 1, 1), st_in),
        ],
        out_specs=(
            pl.BlockSpec((L, H * dv), lambda b, c: (row(b, c), 0)),
            pl.BlockSpec((1, H, dk, dv), st4),
            pl.BlockSpec((1, H, 1, dk), st4),
            pl.BlockSpec((1, H, 1, 1), st4),
        ),
        out_shape=(
            jax.ShapeDtypeStruct((B * S, out_cols), BF16),
            jax.ShapeDtypeStruct((B, H, dk, dv), F32),
            jax.ShapeDtypeStruct((B, H, 1, dk), F32),
            jax.ShapeDtypeStruct((B, H, 1, 1), F32),
        ),
        compiler_params=_cparams(("parallel", "arbitrary")),
        name="mlstm",
    )(gate_b, gates, proj, proj, proj, proj, norm_g.reshape(1, H * dv), c0, n0, m0)


def _attn_prompt_kernel(*refs, dil, heads, head0, first, last, with_prev):
    bc_ref, bp_ref, q_ref, kc_ref, vc_ref = refs[:5]
    refs = refs[5:]
    if with_prev:
        kp_ref, vp_ref = refs[:2]
        refs = refs[2:]
    if not first:
        num_in_ref, sm_in_ref = refs[:2]
        refs = refs[2:]
    if last:
        _, out_ref, o32_ref = refs
    else:
        num_out_ref, sm_out_ref = refs
    has_prev = pl.program_id(2) > 0
    T = A_J
    lane = lax.broadcasted_iota(jnp.int32, (T, 128), 1)
    nt = (((1,), (1,)), ((), ()))
    for r in range(dil):
        rows = pl.ds(r, T, stride=dil) if dil > 1 else pl.ds(0, T)
        sm_old = None if first else sm_in_ref[rows, :]
        sm_new = jnp.zeros((T, 128), F32)
        for hl in range(heads):
            hg = head0 + hl
            qh = (q_ref[hl, rows, :] * (A_DH ** -0.5)).astype(BF16)
            lc = lax.dot_general(qh, kc_ref[hl, rows, :].astype(BF16), nt, preferred_element_type=F32) + bc_ref[hg]
            mx = jnp.max(lc, axis=1, keepdims=True)
            if with_prev:
                lp = lax.dot_general(qh, kp_ref[hl, rows, :].astype(BF16), nt, preferred_element_type=F32) + bp_ref[hg]
                lp = jnp.where(has_prev, lp, NEG_INF)
                mx = jnp.maximum(mx, jnp.max(lp, axis=1, keepdims=True))
            pc = jnp.exp(lc - mx)
            s = jnp.sum(pc, axis=1, keepdims=True)
            o = jnp.dot(pc.astype(BF16), vc_ref[hl, rows, :].astype(BF16), preferred_element_type=F32)
            if with_prev:
                pp = jnp.exp(lp - mx)
                s = s + jnp.sum(pp, axis=1, keepdims=True)
                o = o + jnp.dot(pp.astype(BF16), vp_ref[hl, rows, :].astype(BF16), preferred_element_type=F32)
            if not first:
                s_old = sm_old[:, hg:hg + 1]
                m_old = sm_old[:, A_HEADS + hg:A_HEADS + hg + 1]
                m_all = jnp.maximum(m_old, mx)
                a_old = jnp.exp(m_old - m_all)
                a_new = jnp.exp(mx - m_all)
                o = a_old * num_in_ref[hl, rows, :] + a_new * o
                s = a_old * s_old + a_new * s
                mx = m_all
            if last:
                o32_ref[hl, rows, :] = o / s
            else:
                num_out_ref[hl, rows, :] = o
                sm_new = jnp.where(lane == hg, s, sm_new)
                sm_new = jnp.where(lane == A_HEADS + hg, mx, sm_new)
        if not last:
            sm_out_ref[rows, :] = sm_new
    if last:
        for hl in range(heads):
            out_ref[:, hl * A_DH:(hl + 1) * A_DH] = o32_ref[hl].astype(out_ref.dtype)


def attn_prompt_branch(att, bias_c, bias_p, state, mix, *, B, S, g, dil, hb):
    first, last = state is None, mix is not None
    span = A_J * dil
    nb = S // span
    with_prev = nb > 1
    nhb = A_HEADS // hb
    assert nhb == 1 or last
    cur = lambda part: pl.BlockSpec((hb, span, A_DH), lambda b, c, n: ((3 * g + part) * nhb + c, b * nb + n, 0))
    prev = lambda part: pl.BlockSpec(
        (hb, span, A_DH), lambda b, c, n: ((3 * g + part) * nhb + c, jnp.maximum(b * nb + n - 1, 0), 0))
    bias_spec = pl.BlockSpec((A_HEADS, A_J, A_J), lambda b, c, n: (0, 0, 0))
    num_spec = pl.BlockSpec((hb, span, A_DH), lambda b, c, n: (c, b * nb + n, 0))
    sm_spec = pl.BlockSpec((span, 128), lambda b, c, n: (b * nb + n, 0))
    in_specs = [bias_spec, bias_spec, cur(0), cur(1), cur(2)]
    args = [bias_c, bias_p, att, att, att]
    if with_prev:
        in_specs += [prev(1), prev(2)]
        args += [att, att]
    if not first:
        in_specs += [num_spec, sm_spec]
        args += list(state)
    aliases = {}
    if last:
        in_specs.append(pl.BlockSpec(memory_space=pl.ANY))
        args.append(mix)
        aliases = {len(args) - 1: 0}
        out_specs = pl.BlockSpec((span, hb * A_DH), lambda b, c, n: (b * nb + n, MIX_ATT_COL * nhb + c))
        out_shape = jax.ShapeDtypeStruct(mix.shape, mix.dtype)
    else:
        out_specs = (num_spec, sm_spec)
        out_shape = (jax.ShapeDtypeStruct((A_HEADS, B * S, A_DH), F32), jax.ShapeDtypeStruct((B * S, 128), F32))

    def body(*refs):
        kw = dict(dil=dil, heads=hb, first=first, last=last, with_prev=with_prev)
        if nhb == 1:
            _attn_prompt_kernel(*refs, head0=0, **kw)
        else:
            for cb in range(nhb):
                @pl.when(pl.program_id(1) == cb)
                def _(cb=cb):
                    _attn_prompt_kernel(*refs, head0=cb * hb, **kw)

    return pl.pallas_call(
        body, grid=(B, nhb, nb), in_specs=in_specs, out_specs=out_specs, out_shape=out_shape,
        input_output_aliases=aliases,
        scratch_shapes=[pltpu.VMEM((hb, span, A_DH), F32)] if last else [],
        compiler_params=_cparams(("parallel", "parallel", "arbitrary")),
        name=f"attn_prompt_{g}",
    )(*args)


def _attn_decode_kernel(bias_ref, a_ref, k1_ref, v1_ref, k2_ref, v2_ref, k3_ref, v3_ref, out_ref):
    caches = ((k1_ref, v1_ref), (k2_ref, v2_ref), (k3_ref, v3_ref))
    nt = (((1,), (1,)), ((), ()))
    hw = A_HEADS * A_DH
    for h in range(A_HEADS):
        outs, dens, maxs = [], [], []
        for g in range(3):
            base = 3 * g * hw + h * A_DH
            qf = (a_ref[0, :, base:base + A_DH] * (A_DH ** -0.5)).astype(BF16)
            kn = a_ref[0, :, base + hw:base + hw + A_DH].astype(BF16).astype(F32)
            vn = a_ref[0, :, base + 2 * hw:base + 2 * hw + A_DH].astype(BF16).astype(F32)
            kc = caches[g][0][:, h, :].astype(BF16)
            vc = caches[g][1][:, h, :].astype(BF16)
            q16 = jnp.broadcast_to(qf, (16, A_DH))
            lc = lax.dot_general(q16, kc, nt, preferred_element_type=F32)[0:1, :] + bias_ref[g, h:h + 1, 0:A_J]
            ln = (jnp.sum(qf.astype(F32) * kn, axis=1, keepdims=True)
                  + bias_ref[g, h:h + 1, A_J:A_J + 1])
            mx = jnp.maximum(jnp.max(lc, axis=1, keepdims=True), ln)
            pc = jnp.exp(lc - mx)
            pn = jnp.exp(ln - mx)
            p16 = jnp.broadcast_to(pc.astype(BF16), (16, A_J))
            o = jnp.dot(p16, vc, preferred_element_type=F32)[0:1, :] + pn.astype(BF16).astype(F32) * vn
            outs.append(o)
            dens.append(jnp.sum(pc, axis=1, keepdims=True) + pn)
            maxs.append(mx)
        m_all = jnp.maximum(jnp.maximum(maxs[0], maxs[1]), maxs[2])
        wts = [jnp.exp(m - m_all) for m in maxs]
        num = wts[0] * outs[0] + wts[1] * outs[1] + wts[2] * outs[2]
        den = wts[0] * dens[0] + wts[1] * dens[1] + wts[2] * dens[2]
        out_ref[0, :, h * A_DH:(h + 1) * A_DH] = num / den


def attn_decode(att, caches, layer, bias_dec):
    B = att.shape[0]
    hw = A_HEADS * A_DH
    in_specs = [pl.BlockSpec(bias_dec.shape, lambda b: (0, 0, 0)),
                pl.BlockSpec((1, 1, N_ATT), lambda b: (b, 0, 0))]
    args = [bias_dec, att.reshape(B, 1, N_ATT)]
    for g, (window, dil) in enumerate(A_BRANCHES):
        for t in caches[2 * g:2 * g + 2]:
            assert t.shape[2] == window
            in_specs.append(pl.BlockSpec((None, None, A_J, None, A_HEADS, A_DH), lambda b: (layer, b, 0, 0, 0, 0)))
            args.append(t.reshape(t.shape[0], B, window // dil, dil, A_HEADS, A_DH))
    out = pl.pallas_call(
        _attn_decode_kernel, grid=(B,), in_specs=in_specs,
        out_specs=pl.BlockSpec((1, 1, hw), lambda b: (b, 0, 0)),
        out_shape=jax.ShapeDtypeStruct((B, 1, hw), F32),
        compiler_params=_cparams(("parallel",)),
        name="attn_decode",
    )(*args)
    return out.reshape(B, hw)


def _shift_kernel(c_ref, new_ref, o_ref):
    P = c_ref.shape[0]
    R = min(COPY_ROWS, P)

    def chunk(i, carry):
        o_ref[pl.ds(i * R, R)] = c_ref[pl.ds(i * R + 1, R)]
        return carry
    lax.fori_loop(0, P // R - 1, chunk, 0)
    o_ref[P - R:P - 1] = c_ref[P - R + 1:P]
    o_ref[P - 1:P] = new_ref[...]


def shift_append(cache, new_rows):
    Dp, B, P, H, Dh = cache.shape
    blk = lambda rows: pl.BlockSpec((None, None, rows, H, Dh), lambda l, b: (l, b, 0, 0, 0))
    return pl.pallas_call(
        _shift_kernel, grid=(Dp, B), in_specs=[blk(P), blk(1)], out_specs=blk(P),
        out_shape=jax.ShapeDtypeStruct(cache.shape, cache.dtype),
        compiler_params=_cparams(("parallel", "parallel")),
        name="shift_append",
    )(cache, new_rows.astype(cache.dtype))


def _conv_kernel(cv_ref, cg_ref, hv_ref, hg_ref, past_ref, cw_ref, cb_ref, lg_ref, lb_ref, *rest,
                 TS, RC, nvalid):
    y_ref, tail_ref, xs_ref = rest[-3:]
    t = pl.program_id(1)
    H = CONV_HALO

    @pl.when(t == 0)
    def _():
        xs_ref[0:H, :] = past_ref[0]

    @pl.when(t > 0)
    def _():
        xs_ref[0:H, :] = hv_ref[...] * jax.nn.sigmoid(hg_ref[...])

    xs_ref[H:H + TS, :] = cv_ref[...] * jax.nn.sigmoid(cg_ref[...])
    lead = H - (C_WIDTH - 1)
    for rc in range(TS // RC):
        acc = jnp.zeros((RC, C_CH), F32)
        for w in range(C_WIDTH):
            r0 = rc * RC + lead + w
            acc = acc + xs_ref[r0:r0 + RC, :] * cw_ref[w:w + 1, :]
        y = _layer_norm_rows(acc + cb_ref[...], lg_ref[...], lb_ref[...])
        y_ref[rc * RC:(rc + 1) * RC, :] = (y * jax.nn.sigmoid(y)).astype(y_ref.dtype)
    tail_ref[0] = xs_ref[nvalid:nvalid + H, :]


def conv_block(proj, past, conv_w, conv_b, ln_g, ln_b, mix, *, B, S, TS, nvalid, vcol):
    nt = S // TS
    H = CONV_HALO
    RC = min(TS, 32)
    past_p = jnp.pad(past.astype(F32), ((0, 0), (H - (C_WIDTH - 1), 0), (0, 0)))
    cur = lambda col: pl.BlockSpec((TS, C_CH), lambda b, t: (b * nt + t, col))
    halo_rows = min(H, B * S)
    halo = lambda col: pl.BlockSpec(
        (halo_rows, C_CH), lambda b, t: (jnp.maximum((b * S + t * TS) // H - 1, 0), col))
    vec = pl.BlockSpec((1, C_CH), lambda b, t: (0, 0))
    in_specs = [cur(vcol), cur(vcol + 1), halo(vcol), halo(vcol + 1),
                pl.BlockSpec((1, H, C_CH), lambda b, t: (b, 0, 0)),
                pl.BlockSpec((H, C_CH), lambda b, t: (0, 0)), vec, vec, vec]
    args = [proj, proj, proj, proj, past_p, jnp.pad(conv_w.astype(F32), ((0, H - C_WIDTH), (0, 0))),
            conv_b.reshape(1, C_CH), ln_g.reshape(1, C_CH), ln_b.reshape(1, C_CH)]
    aliases = {}
    if mix is None:
        y_spec = pl.BlockSpec((TS, C_CH), lambda b, t: (b * nt + t, 0))
        y_shape = jax.ShapeDtypeStruct((B * S, C_CH), BF16)
    else:
        in_specs.append(pl.BlockSpec(memory_space=pl.ANY))
        args.append(mix)
        aliases = {len(args) - 1: 0}
        y_spec = pl.BlockSpec((TS, C_CH), lambda b, t: (b * nt + t, MIX_CONV_COL))
        y_shape = jax.ShapeDtypeStruct(mix.shape, mix.dtype)
    y, tail = pl.pallas_call(
        functools.partial(_conv_kernel, TS=TS, RC=RC, nvalid=nvalid),
        grid=(B, nt), in_specs=in_specs,
        out_specs=(y_spec, pl.BlockSpec((1, H, C_CH), lambda b, t: (b, 0, 0))),
        out_shape=(y_shape, jax.ShapeDtypeStruct((B, H, C_CH), F32)),
        scratch_shapes=[pltpu.VMEM((H + TS, C_CH), F32)],
        input_output_aliases=aliases,
        compiler_params=_cparams(("parallel", "arbitrary")),
        name="conv_ln_silu",
    )(*args)
    return y, tail[:, H - (C_WIDTH - 1):]


def _xattn_kernel(q_ref, mk_ref, mv_ref, o_ref):
    q = q_ref[0]
    logits = lax.dot_general(q, mk_ref[...].astype(BF16), (((1,), (1,)), ((), ())),
                             preferred_element_type=F32)
    e = jnp.exp(logits - jnp.max(logits, axis=-1, keepdims=True))
    p = e / jnp.sum(e, axis=-1, keepdims=True)
    o_ref[0] = jnp.dot(p.astype(BF16), mv_ref[...].astype(BF16), preferred_element_type=F32).astype(o_ref.dtype)


def cross_attn(q, mk, mv, layer, *, TS):
    B, S, _ = q.shape
    TS = min(TS, S)
    mem = pl.BlockSpec((None, None, N_MEM, XA_DH), lambda b, h, t: (layer, b, 0, h))
    return pl.pallas_call(
        _xattn_kernel, grid=(B, XA_HEADS, S // TS),
        in_specs=[pl.BlockSpec((1, TS, XA_DH), lambda b, h, t: (b, t, h)), mem, mem],
        out_specs=pl.BlockSpec((1, TS, XA_DH), lambda b, h, t: (b, t, h)),
        out_shape=jax.ShapeDtypeStruct(q.shape, BF16),
        compiler_params=_cparams(("parallel", "parallel", "arbitrary")),
        name="cross_attn",
    )(q, mk, mv)


def _rel_bucket(dist):
    exact = N_BUCKETS // 2
    lg = jnp.log(jnp.maximum(dist, 1).astype(F32) / exact) / math.log(REL_MAX_DIST / exact)
    large = jnp.minimum(exact + (lg * (N_BUCKETS - exact)).astype(jnp.int32), N_BUCKETS - 1)
    return jnp.where(dist < exact, dist, large)


def _bias_tables(rel_bias):
    J = A_J
    cur, prev, dec = [], [], []
    for g, (window, dil) in enumerate(A_BRANCHES):
        table = rel_bias[:, g * A_HEADS:(g + 1) * A_HEADS].astype(F32)
        bucket = _rel_bucket(dil * jnp.arange(window // dil + 1))
        onehot = bucket[:, None] == jnp.arange(N_BUCKETS)[None, :]
        bias_j = jnp.sum(jnp.where(onehot[:, :, None], table[None], 0.0), axis=1)
        rev = bias_j.T[:, ::-1]
        u = jnp.concatenate([rev, jnp.full((A_HEADS, J), NEG_INF, F32)], axis=1)
        tile = jnp.broadcast_to(u[:, None, :], (A_HEADS, J, 2 * J + 1)).reshape(A_HEADS, J * (2 * J + 1))
        tile = tile[:, :J * 2 * J].reshape(A_HEADS, J, 2 * J)
        prev.append(tile[:, :, :J])
        cur.append(tile[:, :, J:])
        dec.append(jnp.pad(rev, ((0, 0), (0, 2 * J - (J + 1)))))
    return cur, prev, jnp.stack(dec)


def _prep_weights(w_in, w_out, xa_wq, xa_wk, xa_wv, xa_wo, ffn_w1, ffn_w2):
    a0 = N_MAIN + N_GATE
    w_gate = jnp.pad(w_in[:, :, N_MAIN:a0], ((0, 0), (0, 0), (0, GATE_PAD - N_GATE))).astype(BF16)
    w_main = jnp.concatenate([w_in[:, :, :N_MAIN], w_in[:, :, a0 + N_ATT:]], axis=2).astype(BF16)
    w_att = w_in[:, :, a0:a0 + N_ATT].astype(BF16)
    cast = lambda w: w.astype(BF16)
    return dict(main=w_main, att=w_att, gate=w_gate,
                out=cast(w_out), wq=cast(xa_wq), wk=cast(xa_wk), wv=cast(xa_wv),
                wo=cast(xa_wo), w1=cast(ffn_w1), w2=cast(ffn_w2))


def _gate_layout(gmat, B, S, L):
    g = gmat[:, :N_GATE].reshape(B, S // L, L, 2, M_HEADS)
    return g.transpose(0, 3, 4, 1, 2)


def _trunk_tail(x32, x16, mix16, mk, mv, mem_layer, w, l, lg, lb, *, B, S, tm, tr, xa_ts, pad_q):
    mm = functools.partial(matmul, tm=tm, tn=1024, tk=D_MODEL)
    x32, x16 = post_ln(x32, mm(mix16, w["out"], l), lg[0], lb[0], tr=tr)
    q = mm(x16, w["wq"], l, out_dtype=BF16, scale=XA_DH ** -0.5)
    if pad_q > 1:
        qs = jnp.zeros((B, pad_q, D_MODEL), BF16).at[:, 0].set(q[:B])
        o = cross_attn(qs, mk, mv, mem_layer, TS=pad_q)[:, 0]
        o = jnp.pad(o, ((0, q.shape[0] - B), (0, 0)))
    else:
        o = cross_attn(q.reshape(B, S, D_MODEL), mk, mv, mem_layer, TS=xa_ts).reshape(B * S, D_MODEL)
    x32, x16 = post_ln(x32, mm(o, w["wo"], l), lg[1], lb[1], tr=tr)
    hid = mm(x16, w["w1"], l, out_dtype=BF16, epilogue="relu2")
    x32, x16 = post_ln(x32, mm(hid, w["w2"], l), lg[2], lb[2], tr=tr)
    return x32, x16


def kernel(x_prompt, x_sample, mem_prompt, cache_win_k1, cache_win_v1, cache_win_k2, cache_win_v2, cache_win_k3, cache_win_v3, state_mlstm_C, state_mlstm_n, state_mlstm_m, state_conv, cache_mem_k, cache_mem_v, rel_bias, w_in, mlstm_gate_bias, mlstm_norm_g, conv_w, conv_b, conv_ln_g, conv_ln_b, w_out, xa_wq, xa_wk, xa_wv, xa_wo, ffn_w1, ffn_w2, ln_g, ln_b):
    B, S, D = x_prompt.shape
    BS = x_sample.shape[0]
    MS = SAMPLE_ROWS
    L = math.gcd(S, M_CHUNK)
    bias_c, bias_p, bias_dec = _bias_tables(rel_bias)
    cache_win = (cache_win_k1, cache_win_v1, cache_win_k2, cache_win_v2, cache_win_k3, cache_win_v3)
    sample_state = (state_mlstm_C.astype(F32), state_mlstm_n.astype(F32), state_mlstm_m.astype(F32))
    smem_k = cache_mem_k.reshape(DEPTH, BS, N_MEM, D)
    smem_v = cache_mem_v.reshape(DEPTH, BS, N_MEM, D)

    xp32 = x_prompt.reshape(B * S, D)
    xp16 = xp32.astype(BF16)
    xs32 = jnp.pad(x_sample.reshape(BS, D), ((0, MS - BS), (0, 0)))
    xs16 = xs32.astype(BF16)
    mem16 = mem_prompt.reshape(B * N_MEM, D).astype(BF16)

    p_win = [[] for _ in range(6)]
    new_rows = [[] for _ in range(6)]
    p_C, p_n, p_m, p_conv, p_mk, p_mv = [], [], [], [], [], []
    s_C, s_n, s_m, s_conv = [], [], [], []
    w = _prep_weights(w_in, w_out, xa_wq, xa_wk, xa_wv, xa_wo, ffn_w1, ffn_w2)
    mm = functools.partial(matmul, tm=1024, tn=1024, tk=D)
    for l in range(DEPTH):
        gate_b = mlstm_gate_bias[l].astype(F32)

        main = mm(xp16, w["main"], l)
        att = mm(xp16, w["att"], l, head_major=True)
        gmat = mm(xp16, w["gate"], l)
        mix, C1, n1, m1 = mlstm(main, _gate_layout(gmat, B, S, L), gate_b, mlstm_norm_g[l], None, 0,
                                B=B, S=S, L=L, nvalid=L, out_cols=D)
        state = None
        for g, (window, dil) in enumerate(A_BRANCHES):
            last = g == len(A_BRANCHES) - 1
            out = attn_prompt_branch(att, bias_c[g], bias_p[g], state, mix if last else None,
                                     B=B, S=S, g=g, dil=dil, hb=A_HEADS // 2 if last else A_HEADS)
            if last:
                mix = out
            else:
                state = out
            keep = min(window, S)
            for j in range(2):
                h0 = (3 * g + 1 + j) * A_HEADS
                kv = att[h0:h0 + A_HEADS].reshape(A_HEADS, B, S, A_DH)[:, :, S - keep:]
                p_win[2 * g + j].append(kv.transpose(1, 2, 0, 3))
        mix, conv_tail = conv_block(main, jnp.zeros((B, C_WIDTH - 1, C_CH), F32), conv_w[l], conv_b[l],
                                    conv_ln_g[l], conv_ln_b[l], mix, B=B, S=S, TS=128, nvalid=128, vcol=CONV_COL)
        mk = mm(mem16, w["wk"], l)
        mv = mm(mem16, w["wv"], l)
        xp32, xp16 = _trunk_tail(xp32, xp16, mix, mk.reshape(1, B, N_MEM, D), mv.reshape(1, B, N_MEM, D), 0, w, l,
                                 ln_g[l], ln_b[l], B=B, S=S, tm=1024, tr=256, xa_ts=512, pad_q=1)
        p_C.append(C1); p_n.append(n1.reshape(B, M_HEADS, M_DK)); p_m.append(m1.reshape(B, M_HEADS))
        p_conv.append(conv_tail)
        p_mk.append(mk.reshape(B, N_MEM, XA_HEADS, XA_DH)); p_mv.append(mv.reshape(B, N_MEM, XA_HEADS, XA_DH))

        smain = mm(xs16, w["main"], l)
        satt = mm(xs16, w["att"], l)[:BS]
        sgm = mm(xs16, w["gate"], l)
        LS = M_CHUNK
        main_pad = jnp.zeros((BS, LS, N_MAIN), F32).at[:, 0].set(smain[:BS, :N_MAIN]).reshape(BS * LS, N_MAIN)
        gate_pad = jnp.zeros((BS, LS, GATE_PAD), F32).at[:, 0].set(sgm[:BS]).reshape(BS * LS, GATE_PAD)
        shm, Cs, ns, ms = mlstm(main_pad, _gate_layout(gate_pad, BS, LS, LS), gate_b, mlstm_norm_g[l],
                                sample_state, l, B=BS, S=LS, L=LS, nvalid=1, out_cols=M_HEADS * M_DV)
        shm = shm.reshape(BS, LS, M_HEADS * M_DV)[:, 0]
        sha = attn_decode(satt, cache_win, l, bias_dec)
        for g in range(len(A_BRANCHES)):
            for j in range(2):
                c0 = (3 * g + 1 + j) * COL_BLK
                new_rows[2 * g + j].append(satt[:, c0:c0 + COL_BLK].reshape(BS, 1, A_HEADS, A_DH))
        CT = 8
        cpad = jnp.zeros((BS, CT, 2 * C_CH), F32).at[:, 0].set(smain[:BS, N_MAIN:]).reshape(BS * CT, 2 * C_CH)
        syc, sconv_tail = conv_block(cpad, state_conv[l], conv_w[l], conv_b[l], conv_ln_g[l], conv_ln_b[l], None,
                                     B=BS, S=CT, TS=CT, nvalid=1, vcol=0)
        syc = syc.reshape(BS, CT, C_CH)[:, 0]
        smix = jnp.concatenate([shm, sha.astype(BF16), syc], axis=1)
        smix = jnp.pad(smix, ((0, MS - BS), (0, 0)))
        xs32, xs16 = _trunk_tail(xs32, xs16, smix, smem_k, smem_v, l, w, l, ln_g[l], ln_b[l],
                                 B=BS, S=1, tm=MS, tr=MS, xa_ts=MS, pad_q=MS)
        s_C.append(Cs); s_n.append(ns.reshape(BS, M_HEADS, M_DK)); s_m.append(ms.reshape(BS, M_HEADS))
        s_conv.append(sconv_tail)

    st = lambda xs: jnp.stack(xs)
    s_win = [shift_append(cache_win[i], st(new_rows[i])) for i in range(6)]
    return (xp32.reshape(B, S, D), xs32[:BS].reshape(BS, 1, D),
            st(p_win[0]), st(p_win[1]), st(p_win[2]), st(p_win[3]), st(p_win[4]), st(p_win[5]),
            st(p_C), st(p_n), st(p_m), st(p_conv), st(p_mk), st(p_mv),
            s_win[0], s_win[1], s_win[2], s_win[3], s_win[4], s_win[5],
            st(s_C), st(s_n), st(s_m), st(s_conv))
```

```python
import functools
import math

import jax
import jax.numpy as jnp
from jax import lax
from jax.experimental import pallas as pl
from jax.experimental.pallas import tpu as pltpu

F32 = jnp.float32
BF16 = jnp.bfloat16

D_MODEL = 4096
DEPTH = 2
M_HEADS, M_DK, M_DV, M_CHUNK = 4, 256, 512, 64
A_HEADS, A_DH = 8, 128
A_BRANCHES = ((128, 1), (512, 4), (2048, 16))
A_J = 128
C_CH, C_WIDTH = 1024, 31
N_BUCKETS, REL_MAX_DIST = 32, 2048
XA_HEADS = 4
XA_DH = D_MODEL // XA_HEADS
N_MEM = 256
D_FF = 4 * D_MODEL
DEEPNORM_ALPHA = (2 * DEPTH) ** 0.25
LN_EPS = 1e-5

N_MAIN = 2 * M_HEADS * M_DK + 2 * M_HEADS * M_DV
N_GATE = 2 * M_HEADS
N_ATT = 9 * A_HEADS * A_DH
COL_BLK = 1024
CONV_COL = N_MAIN // COL_BLK
MIX_ATT_COL = M_HEADS * M_DV // COL_BLK
MIX_CONV_COL = MIX_ATT_COL + 1
GATE_PAD = 128
CONV_HALO = 32
LN_ROWS = 16
SAMPLE_ROWS = 16
COPY_ROWS = 64

VMEM_LIMIT = 56 * 1024 * 1024
NEG_INF = float("-inf")


def _cparams(sem):
    return pltpu.CompilerParams(dimension_semantics=sem, vmem_limit_bytes=VMEM_LIMIT)


def _layer_norm_rows(z, g, b):
    mu = jnp.mean(z, axis=-1, keepdims=True)
    zc = z - mu
    var = jnp.mean(zc * zc, axis=-1, keepdims=True)
    return zc * lax.rsqrt(var + LN_EPS) * g + b


def _mm_kernel(*refs, nk, epilogue, scale, head_major, has_rider):
    if has_rider:
        x_ref, r_ref, w_ref, o_ref, ro_ref = refs[:5]
        scratch = refs[5:]
    else:
        x_ref, w_ref, o_ref = refs[:3]
        scratch = refs[3:]

    def finish(acc, out_ref, as_head_major):
        if epilogue == "relu2":
            r = jnp.maximum(acc, 0.0)
            acc = r * r
        elif scale != 1.0:
            acc = acc * scale
        if as_head_major:
            for c in range(out_ref.shape[0]):
                out_ref[c] = acc[:, c * 128:(c + 1) * 128].astype(out_ref.dtype)
        else:
            out_ref[...] = acc.astype(out_ref.dtype)

    def accumulate(lhs_ref, out_ref, acc_ref, as_head_major):
        prod = lambda: jnp.dot(lhs_ref[...].astype(BF16), w_ref[...], preferred_element_type=F32)
        if nk == 1:
            finish(prod(), out_ref, as_head_major)
            return
        k = pl.program_id(2)

        @pl.when(k == 0)
        def _():
            acc_ref[...] = prod()

        @pl.when(k > 0)
        def _():
            acc_ref[...] += prod()

        @pl.when(k == nk - 1)
        def _():
            finish(acc_ref[...], out_ref, as_head_major)

    accumulate(x_ref, o_ref, scratch[0] if nk > 1 else None, head_major)
    if has_rider:
        @pl.when(pl.program_id(1) == 0)
        def _():
            accumulate(r_ref, ro_ref, scratch[1] if nk > 1 else None, False)


def matmul(x, w, layer, *, tm, tn, tk, out_dtype=F32, epilogue="none", scale=1.0, head_major=False, rider=None):
    M, K = x.shape
    N = w.shape[2]
    tm, tn, tk = min(tm, M), min(tn, N), min(tk, K)
    assert M % tm == 0 and N % tn == 0 and K % tk == 0, (x.shape, w.shape, tm, tn, tk)
    nk = K // tk
    if head_major:
        out_specs = pl.BlockSpec((tn // 128, tm, 128), lambda j, i, k: (j, i, 0))
        out_shape = jax.ShapeDtypeStruct((N // 128, M, 128), out_dtype)
    else:
        out_specs = pl.BlockSpec((tm, tn), lambda j, i, k: (i, j))
        out_shape = jax.ShapeDtypeStruct((M, N), out_dtype)
    x_spec = pl.BlockSpec((tm, tk), lambda j, i, k: (i, k))
    w_spec = pl.BlockSpec((None, tk, tn), lambda j, i, k: (layer, k, j))
    scratch = [pltpu.VMEM((tm, tn), F32)] if nk > 1 else []
    if rider is None:
        in_specs, args = [x_spec, w_spec], (x, w)
    else:
        R = rider.shape[0]
        in_specs, args = [x_spec, pl.BlockSpec((R, tk), lambda j, i, k: (0, k)), w_spec], (x, rider, w)
        out_specs = (out_specs, pl.BlockSpec((R, tn), lambda j, i, k: (0, j)))
        out_shape = (out_shape, jax.ShapeDtypeStruct((R, N), out_dtype))
        scratch += [pltpu.VMEM((R, tn), F32)] if nk > 1 else []
    return pl.pallas_call(
        functools.partial(_mm_kernel, nk=nk, epilogue=epilogue, scale=scale, head_major=head_major,
                          has_rider=rider is not None),
        grid=(N // tn, M // tm, nk),
        in_specs=in_specs, out_specs=out_specs, out_shape=out_shape, scratch_shapes=scratch,
        compiler_params=_cparams(("parallel", "arbitrary", "arbitrary")),
        name=f"mm_{epilogue}",
    )(*args)


def _post_ln_kernel(res_ref, sub_ref, g_ref, b_ref, o32_ref, o16_ref):
    def rows(i, carry):
        r = pl.ds(pl.multiple_of(i * LN_ROWS, LN_ROWS), LN_ROWS)
        y = _layer_norm_rows(DEEPNORM_ALPHA * res_ref[r, :] + sub_ref[r, :], g_ref[...], b_ref[...])
        o32_ref[r, :] = y
        o16_ref[r, :] = y.astype(BF16)
        return carry
    lax.fori_loop(0, o32_ref.shape[0] // LN_ROWS, rows, 0, unroll=2)


def post_ln(res, sub, g, b, *, tr):
    M, N = res.shape
    tr = min(tr, M)
    assert M % tr == 0 and tr % (2 * LN_ROWS) == 0 or tr == LN_ROWS
    row = pl.BlockSpec((tr, N), lambda i: (i, 0))
    vec = pl.BlockSpec((1, N), lambda i: (0, 0))
    return pl.pallas_call(
        _post_ln_kernel, grid=(M // tr,), in_specs=[row, row, vec, vec], out_specs=(row, row),
        out_shape=(jax.ShapeDtypeStruct((M, N), F32), jax.ShapeDtypeStruct((M, N), BF16)),
        compiler_params=_cparams(("parallel",)),
        name="post_ln",
    )(res, sub, g.reshape(1, N).astype(F32), b.reshape(1, N).astype(F32))


def _mlstm_kernel(gb_ref, g_ref, q_ref, k_ref, v_ref, o_ref, ng_ref, c0_ref, n0_ref, m0_ref,
                  h_ref, c_ref, n_ref, m_ref, *, L, nvalid):
    c = pl.program_id(1)

    @pl.when(c == 0)
    def _():
        c_ref[...] = c0_ref[...]
        n_ref[...] = n0_ref[...]
        m_ref[...] = m0_ref[...]

    for hd in range(M_HEADS):
        _mlstm_head(hd, c, gb_ref, g_ref, q_ref, k_ref, v_ref, o_ref, ng_ref, h_ref, c_ref, n_ref, m_ref,
                    L=L, nvalid=nvalid)
    if h_ref.shape[1] > M_HEADS * M_DV:
        h_ref[:, M_HEADS * M_DV:] = jnp.zeros((L, h_ref.shape[1] - M_HEADS * M_DV), h_ref.dtype)


def _mlstm_head(hd, c, gb_ref, g_ref, q_ref, k_ref, v_ref, o_ref, ng_ref, h_ref, c_ref, n_ref, m_ref, *, L, nvalid):
    dk, dv = M_DK, M_DV
    ks, vs = slice(hd * dk, (hd + 1) * dk), slice(hd * dv, (hd + 1) * dv)
    ig = g_ref[0, 0, hd, pl.ds(c, 1), :] + gb_ref[0, hd]
    fg = g_ref[0, 1, hd, pl.ds(c, 1), :] + gb_ref[1, hd]
    lf = -(jnp.maximum(-fg, 0.0) + jnp.log1p(jnp.exp(-jnp.abs(fg))))
    if nvalid < L:
        lane = lax.broadcasted_iota(jnp.int32, (1, L), 1)
        ig = jnp.where(lane < nvalid, ig, NEG_INF)
        lf = jnp.where(lane < nvalid, lf, 0.0)
    qi = lax.broadcasted_iota(jnp.int32, (L, L), 0)
    si = lax.broadcasted_iota(jnp.int32, (L, L), 1)
    eye = qi == si
    tri = si <= qi
    lf_col = jnp.sum(jnp.where(eye, lf, 0.0), axis=1, keepdims=True)
    b_col = jnp.sum(jnp.where(tri, lf, 0.0), axis=1, keepdims=True)
    b_row = jnp.sum(jnp.where(qi <= si, lf_col, 0.0), axis=0, keepdims=True)
    m_prev = m_ref[0, hd]
    dmat = jnp.where(tri, b_col - b_row + ig, NEG_INF)
    inter = b_col + m_prev
    mt = jnp.maximum(inter, jnp.max(dmat, axis=1, keepdims=True))
    w_intra = jnp.exp(dmat - mt)
    w_inter = jnp.exp(inter - mt)

    q = q_ref[:, ks]
    kf = k_ref[:, ks] * (M_DK ** -0.5)
    qb, kb, vb = q.astype(BF16), kf.astype(BF16), v_ref[:, vs].astype(BF16)
    c_prev = c_ref[0, hd]
    n_prev = n_ref[0, hd]
    sc = lax.dot_general(qb, kb, (((1,), (1,)), ((), ())), preferred_element_type=F32) * w_intra
    num = (w_inter * jnp.dot(qb, c_prev.astype(BF16), preferred_element_type=F32)
           + jnp.dot(sc.astype(BF16), vb, preferred_element_type=F32))
    den = w_inter * jnp.sum(q * n_prev, axis=1, keepdims=True) + jnp.sum(sc, axis=1, keepdims=True)
    h = num / jnp.maximum(jnp.abs(den), jnp.exp(-mt))

    m_new = mt[L - 1:L, :]
    b_last = b_col[L - 1:L, :]
    w_end = jnp.exp(b_last - b_row + ig - m_new)
    w_end_col = jnp.sum(jnp.where(eye, w_end, 0.0), axis=1, keepdims=True)
    decay = jnp.exp(b_last + m_prev - m_new)
    kw = kf * w_end_col
    c_ref[0, hd] = decay * c_prev + lax.dot_general(
        kw.astype(BF16), vb, (((0,), (0,)), ((), ())), preferred_element_type=F32)
    n_ref[0, hd] = decay * n_prev + jnp.sum(kw, axis=0, keepdims=True)
    m_ref[0, hd] = m_new

    hn = h * lax.rsqrt(jnp.mean(h * h, axis=-1, keepdims=True) + LN_EPS) * ng_ref[:, vs]
    h_ref[:, vs] = (jax.nn.sigmoid(o_ref[:, vs]) * hn).astype(h_ref.dtype)


def mlstm(proj, gates, gate_b, norm_g, state, layer, *, B, S, L, nvalid, out_cols):
    nc = S // L
    H, dk, dv = M_HEADS, M_DK, M_DV
    if state is None:
        c0 = jnp.zeros((1, B, H, dk, dv), F32)
        n0 = jnp.zeros((1, B, H, 1, dk), F32)
        m0 = jnp.zeros((1, B, H, 1, 1), F32)
        layer = 0
    else:
        c0 = state[0]
        n0 = state[1].reshape(state[1].shape[0], B, H, 1, dk)
        m0 = state[2].reshape(state[2].shape[0], B, H, 1, 1)
    st_in = lambda b, c: (layer, b, 0, 0, 0)
    st4 = lambda b, c: (b, 0, 0, 0)
    row = lambda b, c: b * nc + c
    return pl.pallas_call(
        functools.partial(_mlstm_kernel, L=L, nvalid=nvalid),
        grid=(B, nc),
        in_specs=[
            pl.BlockSpec(memory_space=pltpu.SMEM),
            pl.BlockSpec((1, 2, H, nc, L), lambda b, c: (b, 0, 0, 0, 0)),
            pl.BlockSpec((L, H * dk), lambda b, c: (row(b, c), 0)),
            pl.BlockSpec((L, H * dk), lambda b, c: (row(b, c), 1)),
            pl.BlockSpec((L, H * dv), lambda b, c: (row(b, c), 1)),
            pl.BlockSpec((L, H * dv), lambda b, c: (row(b, c), 2)),
            pl.BlockSpec((1, H * dv), lambda b, c: (0, 0)),
            pl.BlockSpec((None, 1, H, dk, dv), st_in),
            pl.BlockSpec((None, 1, H, 1, dk), st_in),
            pl.BlockSpec((None, 1, H, 1, 1), st_in),
        ],
        out_specs=(
            pl.BlockSpec((L, out_cols), lambda b, c: (row(b, c), 0)),
            pl.BlockSpec((1, H, dk, dv), st4),
            pl.BlockSpec((1, H, 1, dk), st4),
            pl.BlockSpec((1, H, 1, 1), st4),
        ),
        out_shape=(
            jax.ShapeDtypeStruct((B * S, out_cols), BF16),
            jax.ShapeDtypeStruct((B, H, dk, dv), F32),
            jax.ShapeDtypeStruct((B, H, 1, dk), F32),
            jax.ShapeDtypeStruct((B, H, 1, 1), F32),
        ),
        compiler_params=_cparams(("parallel", "arbitrary")),
        name="mlstm",
    )(gate_b, gates, proj, proj, proj, proj, norm_g.reshape(1, H * dv), c0, n0, m0)


def _attn_prompt_kernel(*refs, dil, heads, head0, first, last, with_prev):
    bias_ref, q_ref, kc_ref, vc_ref = refs[:4]
    refs = refs[4:]
    if with_prev:
        kp_ref, vp_ref = refs[:2]
        refs = refs[2:]
    if not first:
        num_in_ref, sm_in_ref = refs[:2]
        refs = refs[2:]
    if last:
        _, out_ref, o32_ref = refs
    else:
        num_out_ref, sm_out_ref = refs
    T = A_J
    lane = lax.broadcasted_iota(jnp.int32, (T, 128), 1)
    if with_prev:
        key_ok = (pl.program_id(2) > 0) | (lax.broadcasted_iota(jnp.int32, (T, 2 * T), 1) >= T)
    nt = (((1,), (1,)), ((), ()))
    for r in range(dil):
        rows = pl.ds(r, T, stride=dil) if dil > 1 else pl.ds(0, T)
        sm_old = None if first else sm_in_ref[rows, :]
        sm_new = jnp.zeros((T, 128), F32)
        for hl in range(heads):
            hg = head0 + hl
            qh = (q_ref[hl, rows, :] * (A_DH ** -0.5)).astype(BF16)
            if with_prev:
                kk = jnp.concatenate([kp_ref[hl, rows, :], kc_ref[hl, rows, :]], axis=0).astype(BF16)
                vv = jnp.concatenate([vp_ref[hl, rows, :], vc_ref[hl, rows, :]], axis=0).astype(BF16)
                logits = lax.dot_general(qh, kk, nt, preferred_element_type=F32) + bias_ref[hg]
                logits = jnp.where(key_ok, logits, NEG_INF)
            else:
                kk = kc_ref[hl, rows, :].astype(BF16)
                vv = vc_ref[hl, rows, :].astype(BF16)
                logits = lax.dot_general(qh, kk, nt, preferred_element_type=F32) + bias_ref[hg]
            mx = jnp.max(logits, axis=1, keepdims=True)
            p = jnp.exp(logits - mx)
            s = jnp.sum(p, axis=1, keepdims=True)
            o = jnp.dot(p.astype(BF16), vv, preferred_element_type=F32)
            if not first:
                s_old = sm_old[:, hg:hg + 1]
                m_old = sm_old[:, A_HEADS + hg:A_HEADS + hg + 1]
                m_all = jnp.maximum(m_old, mx)
                a_old = jnp.exp(m_old - m_all)
                a_new = jnp.exp(mx - m_all)
                o = a_old * num_in_ref[hl, rows, :] + a_new * o
                s = a_old * s_old + a_new * s
                mx = m_all
            if last:
                o32_ref[hl, rows, :] = o / s
            else:
                num_out_ref[hl, rows, :] = o
                sm_new = jnp.where(lane == hg, s, sm_new)
                sm_new = jnp.where(lane == A_HEADS + hg, mx, sm_new)
        if not last:
            sm_out_ref[rows, :] = sm_new
    if last:
        for hl in range(heads):
            out_ref[:, hl * A_DH:(hl + 1) * A_DH] = o32_ref[hl].astype(out_ref.dtype)


def attn_prompt_branch(att, bias_c, bias_p, state, mix, *, B, S, g, dil, hb):
    first, last = state is None, mix is not None
    span = A_J * dil
    nb = S // span
    with_prev = nb > 1
    nhb = A_HEADS // hb
    assert nhb == 1 or last
    cur = lambda part: pl.BlockSpec((hb, span, A_DH), lambda b, c, n: ((3 * g + part) * nhb + c, b * nb + n, 0))
    prev = lambda part: pl.BlockSpec(
        (hb, span, A_DH), lambda b, c, n: ((3 * g + part) * nhb + c, jnp.maximum(b * nb + n - 1, 0), 0))
    bias = jnp.concatenate([bias_p, bias_c], axis=2) if with_prev else bias_c
    num_spec = pl.BlockSpec((hb, span, A_DH), lambda b, c, n: (c, b * nb + n, 0))
    sm_spec = pl.BlockSpec((span, 128), lambda b, c, n: (b * nb + n, 0))
    in_specs = [pl.BlockSpec(bias.shape, lambda b, c, n: (0, 0, 0)), cur(0), cur(1), cur(2)]
    args = [bias, att, att, att]
    if with_prev:
        in_specs += [prev(1), prev(2)]
        args += [att, att]
    if not first:
        in_specs += [num_spec, sm_spec]
        args += list(state)
    aliases = {}
    if last:
        in_specs.append(pl.BlockSpec(memory_space=pl.ANY))
        args.append(mix)
        aliases = {len(args) - 1: 0}
        out_specs = pl.BlockSpec((span, hb * A_DH), lambda b, c, n: (b * nb + n, MIX_ATT_COL * nhb + c))
        out_shape = jax.ShapeDtypeStruct(mix.shape, mix.dtype)
    else:
        out_specs = (num_spec, sm_spec)
        out_shape = (jax.ShapeDtypeStruct((A_HEADS, B * S, A_DH), F32), jax.ShapeDtypeStruct((B * S, 128), F32))

    def body(*refs):
        kw = dict(dil=dil, heads=hb, first=first, last=last, with_prev=with_prev)
        if nhb == 1:
            _attn_prompt_kernel(*refs, head0=0, **kw)
        else:
            for cb in range(nhb):
                @pl.when(pl.program_id(1) == cb)
                def _(cb=cb):
                    _attn_prompt_kernel(*refs, head0=cb * hb, **kw)

    return pl.pallas_call(
        body, grid=(B, nhb, nb), in_specs=in_specs, out_specs=out_specs, out_shape=out_shape,
        input_output_aliases=aliases,
        scratch_shapes=[pltpu.VMEM((hb, span, A_DH), F32)] if last else [],
        compiler_params=_cparams(("parallel", "parallel", "arbitrary")),
        name=f"attn_prompt_{g}",
    )(*args)


def _attn_decode_kernel(bias_ref, a_ref, k1_ref, v1_ref, k2_ref, v2_ref, k3_ref, v3_ref, out_ref):
    caches = ((k1_ref, v1_ref), (k2_ref, v2_ref), (k3_ref, v3_ref))
    nt = (((1,), (1,)), ((), ()))
    hw = A_HEADS * A_DH
    for h in range(A_HEADS):
        outs, dens, maxs = [], [], []
        for g in range(3):
            base = 3 * g * hw + h * A_DH
            qf = (a_ref[0, :, base:base + A_DH] * (A_DH ** -0.5)).astype(BF16)
            kn = a_ref[0, :, base + hw:base + hw + A_DH].astype(BF16).astype(F32)
            vn = a_ref[0, :, base + 2 * hw:base + 2 * hw + A_DH].astype(BF16).astype(F32)
            kc = caches[g][0][:, h, :].astype(BF16)
            vc = caches[g][1][:, h, :].astype(BF16)
            q16 = jnp.broadcast_to(qf, (16, A_DH))
            lc = lax.dot_general(q16, kc, nt, preferred_element_type=F32)[0:1, :] + bias_ref[g, h:h + 1, 0:A_J]
            ln = (jnp.sum(qf.astype(F32) * kn, axis=1, keepdims=True)
                  + bias_ref[g, h:h + 1, A_J:A_J + 1])
            mx = jnp.maximum(jnp.max(lc, axis=1, keepdims=True), ln)
            pc = jnp.exp(lc - mx)
            pn = jnp.exp(ln - mx)
            p16 = jnp.broadcast_to(pc.astype(BF16), (16, A_J))
            o = jnp.dot(p16, vc, preferred_element_type=F32)[0:1, :] + pn.astype(BF16).astype(F32) * vn
            outs.append(o)
            dens.append(jnp.sum(pc, axis=1, keepdims=True) + pn)
            maxs.append(mx)
        m_all = jnp.maximum(jnp.maximum(maxs[0], maxs[1]), maxs[2])
        wts = [jnp.exp(m - m_all) for m in maxs]
        num = wts[0] * outs[0] + wts[1] * outs[1] + wts[2] * outs[2]
        den = wts[0] * dens[0] + wts[1] * dens[1] + wts[2] * dens[2]
        out_ref[0, :, h * A_DH:(h + 1) * A_DH] = num / den


def attn_decode(att, caches, layer, bias_dec):
    B = att.shape[0]
    hw = A_HEADS * A_DH
    in_specs = [pl.BlockSpec(bias_dec.shape, lambda b: (0, 0, 0)),
                pl.BlockSpec((1, 1, N_ATT), lambda b: (b, 0, 0))]
    args = [bias_dec, att.reshape(B, 1, N_ATT)]
    for g, (window, dil) in enumerate(A_BRANCHES):
        for t in caches[2 * g:2 * g + 2]:
            assert t.shape[2] == window
            in_specs.append(pl.BlockSpec((None, None, A_J, None, A_HEADS, A_DH), lambda b: (layer, b, 0, 0, 0, 0)))
            args.append(t.reshape(t.shape[0], B, window // dil, dil, A_HEADS, A_DH))
    out = pl.pallas_call(
        _attn_decode_kernel, grid=(B,), in_specs=in_specs,
        out_specs=pl.BlockSpec((1, 1, hw), lambda b: (b, 0, 0)),
        out_shape=jax.ShapeDtypeStruct((B, 1, hw), F32),
        compiler_params=_cparams(("parallel",)),
        name="attn_decode",
    )(*args)
    return out.reshape(B, hw)


def _shift_kernel(c_ref, new_ref, o_ref):
    P = c_ref.shape[0]
    R = min(COPY_ROWS, P)

    def chunk(i, carry):
        o_ref[pl.ds(i * R, R)] = c_ref[pl.ds(i * R + 1, R)]
        return carry
    lax.fori_loop(0, P // R - 1, chunk, 0)
    o_ref[P - R:P - 1] = c_ref[P - R + 1:P]
    o_ref[P - 1:P] = new_ref[...]


def shift_append(cache, new_rows):
    Dp, B, P, H, Dh = cache.shape
    blk = lambda rows: pl.BlockSpec((None, None, rows, H, Dh), lambda l, b: (l, b, 0, 0, 0))
    return pl.pallas_call(
        _shift_kernel, grid=(Dp, B), in_specs=[blk(P), blk(1)], out_specs=blk(P),
        out_shape=jax.ShapeDtypeStruct(cache.shape, cache.dtype),
        compiler_params=_cparams(("parallel", "parallel")),
        name="shift_append",
    )(cache, new_rows.astype(cache.dtype))


def _conv_kernel(cv_ref, cg_ref, hv_ref, hg_ref, past_ref, cw_ref, cb_ref, lg_ref, lb_ref, *rest,
                 TS, RC, nvalid):
    y_ref, tail_ref, xs_ref = rest[-3:]
    t = pl.program_id(1)
    H = CONV_HALO

    @pl.when(t == 0)
    def _():
        xs_ref[0:H, :] = past_ref[0]

    @pl.when(t > 0)
    def _():
        xs_ref[0:H, :] = hv_ref[...] * jax.nn.sigmoid(hg_ref[...])

    xs_ref[H:H + TS, :] = cv_ref[...] * jax.nn.sigmoid(cg_ref[...])
    lead = H - (C_WIDTH - 1)
    for rc in range(TS // RC):
        acc = jnp.zeros((RC, C_CH), F32)
        for w in range(C_WIDTH):
            r0 = rc * RC + lead + w
            acc = acc + xs_ref[r0:r0 + RC, :] * cw_ref[w:w + 1, :]
        y = _layer_norm_rows(acc + cb_ref[...], lg_ref[...], lb_ref[...])
        y_ref[rc * RC:(rc + 1) * RC, :] = (y * jax.nn.sigmoid(y)).astype(y_ref.dtype)
    tail_ref[0] = xs_ref[nvalid:nvalid + H, :]


def conv_block(proj, past, conv_w, conv_b, ln_g, ln_b, mix, *, B, S, TS, nvalid, vcol):
    nt = S // TS
    H = CONV_HALO
    RC = min(TS, 32)
    past_p = jnp.pad(past.astype(F32), ((0, 0), (H - (C_WIDTH - 1), 0), (0, 0)))
    cur = lambda col: pl.BlockSpec((TS, C_CH), lambda b, t: (b * nt + t, col))
    halo_rows = min(H, B * S)
    halo = lambda col: pl.BlockSpec(
        (halo_rows, C_CH), lambda b, t: (jnp.maximum((b * S + t * TS) // H - 1, 0), col))
    vec = pl.BlockSpec((1, C_CH), lambda b, t: (0, 0))
    in_specs = [cur(vcol), cur(vcol + 1), halo(vcol), halo(vcol + 1),
                pl.BlockSpec((1, H, C_CH), lambda b, t: (b, 0, 0)),
                pl.BlockSpec((H, C_CH), lambda b, t: (0, 0)), vec, vec, vec]
    args = [proj, proj, proj, proj, past_p, jnp.pad(conv_w.astype(F32), ((0, H - C_WIDTH), (0, 0))),
            conv_b.reshape(1, C_CH), ln_g.reshape(1, C_CH), ln_b.reshape(1, C_CH)]
    aliases = {}
    if mix is None:
        y_spec = pl.BlockSpec((TS, C_CH), lambda b, t: (b * nt + t, 0))
        y_shape = jax.ShapeDtypeStruct((B * S, C_CH), BF16)
    else:
        in_specs.append(pl.BlockSpec(memory_space=pl.ANY))
        args.append(mix)
        aliases = {len(args) - 1: 0}
        y_spec = pl.BlockSpec((TS, C_CH), lambda b, t: (b * nt + t, MIX_CONV_COL))
        y_shape = jax.ShapeDtypeStruct(mix.shape, mix.dtype)
    y, tail = pl.pallas_call(
        functools.partial(_conv_kernel, TS=TS, RC=RC, nvalid=nvalid),
        grid=(B, nt), in_specs=in_specs,
        out_specs=(y_spec, pl.BlockSpec((1, H, C_CH), lambda b, t: (b, 0, 0))),
        out_shape=(y_shape, jax.ShapeDtypeStruct((B, H, C_CH), F32)),
        scratch_shapes=[pltpu.VMEM((H + TS, C_CH), F32)],
        input_output_aliases=aliases,
        compiler_params=_cparams(("parallel", "arbitrary")),
        name="conv_ln_silu",
    )(*args)
    return y, tail[:, H - (C_WIDTH - 1):]


def _xattn_kernel(q_ref, mk_ref, mv_ref, o_ref):
    q = q_ref[0]
    logits = lax.dot_general(q, mk_ref[...].astype(BF16), (((1,), (1,)), ((), ())),
                             preferred_element_type=F32)
    e = jnp.exp(logits - jnp.max(logits, axis=-1, keepdims=True))
    p = e / jnp.sum(e, axis=-1, keepdims=True)
    o_ref[0] = jnp.dot(p.astype(BF16), mv_ref[...].astype(BF16), preferred_element_type=F32).astype(o_ref.dtype)


def cross_attn(q, mk, mv, layer, *, TS):
    B, S, _ = q.shape
    TS = min(TS, S)
    mem = pl.BlockSpec((None, None, N_MEM, XA_DH), lambda b, h, t: (layer, b, 0, h))
    return pl.pallas_call(
        _xattn_kernel, grid=(B, XA_HEADS, S // TS),
        in_specs=[pl.BlockSpec((1, TS, XA_DH), lambda b, h, t: (b, t, h)), mem, mem],
        out_specs=pl.BlockSpec((1, TS, XA_DH), lambda b, h, t: (b, t, h)),
        out_shape=jax.ShapeDtypeStruct(q.shape, BF16),
        compiler_params=_cparams(("parallel", "parallel", "arbitrary")),
        name="cross_attn",
    )(q, mk, mv)


def _rel_bucket(dist):
    exact = N_BUCKETS // 2
    lg = jnp.log(jnp.maximum(dist, 1).astype(F32) / exact) / math.log(REL_MAX_DIST / exact)
    large = jnp.minimum(exact + (lg * (N_BUCKETS - exact)).astype(jnp.int32), N_BUCKETS - 1)
    return jnp.where(dist < exact, dist, large)


def _bias_tables(rel_bias):
    J = A_J
    cur, prev, dec = [], [], []
    for g, (window, dil) in enumerate(A_BRANCHES):
        table = rel_bias[:, g * A_HEADS:(g + 1) * A_HEADS].astype(F32)
        bucket = _rel_bucket(dil * jnp.arange(window // dil + 1))
        onehot = bucket[:, None] == jnp.arange(N_BUCKETS)[None, :]
        bias_j = jnp.sum(jnp.where(onehot[:, :, None], table[None], 0.0), axis=1)
        rev = bias_j.T[:, ::-1]
        u = jnp.concatenate([rev, jnp.full((A_HEADS, J), NEG_INF, F32)], axis=1)
        tile = jnp.broadcast_to(u[:, None, :], (A_HEADS, J, 2 * J + 1)).reshape(A_HEADS, J * (2 * J + 1))
        tile = tile[:, :J * 2 * J].reshape(A_HEADS, J, 2 * J)
        prev.append(tile[:, :, :J])
        cur.append(tile[:, :, J:])
        dec.append(jnp.pad(rev, ((0, 0), (0, 2 * J - (J + 1)))))
    return cur, prev, jnp.stack(dec)


def _prep_weights(w_in, w_out, xa_wq, xa_wk, xa_wv, xa_wo, ffn_w1, ffn_w2):
    a0 = N_MAIN + N_GATE
    w_gate = jnp.pad(w_in[:, :, N_MAIN:a0], ((0, 0), (0, 0), (0, GATE_PAD - N_GATE))).astype(BF16)
    w_main = jnp.concatenate([w_in[:, :, :N_MAIN], w_in[:, :, a0 + N_ATT:]], axis=2).astype(BF16)
    w_att = w_in[:, :, a0:a0 + N_ATT].astype(BF16)
    cast = lambda w: w.astype(BF16)
    return dict(main=w_main, att=w_att, gate=w_gate,
                out=cast(w_out), wq=cast(xa_wq), wk=cast(xa_wk), wv=cast(xa_wv),
                wo=cast(xa_wo), w1=cast(ffn_w1), w2=cast(ffn_w2))


def _gate_layout(gmat, B, S, L):
    g = gmat[:, :N_GATE].reshape(B, S // L, L, 2, M_HEADS)
    return g.transpose(0, 3, 4, 1, 2)


def _trunk_tail(xp, xs, mix, smix, mk, mv, smk, smv, w, l, lg, lb, *, B, S, BS):
    mm = functools.partial(matmul, tm=1024, tn=1024, tk=D_MODEL)
    MS = xs[0].shape[0]

    def both_ln(sub, ssub, i):
        return post_ln(xp[0], sub, lg[i], lb[i], tr=256), post_ln(xs[0], ssub, lg[i], lb[i], tr=MS)

    xp, xs = both_ln(*mm(mix, w["out"], l, rider=smix), 0)
    q, sq = mm(xp[1], w["wq"], l, out_dtype=BF16, scale=XA_DH ** -0.5, rider=xs[1])
    o = cross_attn(q.reshape(B, S, D_MODEL), mk, mv, 0, TS=512).reshape(B * S, D_MODEL)
    sq = jnp.zeros((BS, MS, D_MODEL), BF16).at[:, 0].set(sq[:BS])
    so = jnp.pad(cross_attn(sq, smk, smv, l, TS=MS)[:, 0], ((0, MS - BS), (0, 0)))
    xp, xs = both_ln(*mm(o, w["wo"], l, rider=so), 1)
    hid, shid = mm(xp[1], w["w1"], l, out_dtype=BF16, epilogue="relu2", rider=xs[1])
    xp, xs = both_ln(*mm(hid, w["w2"], l, rider=shid), 2)
    return xp, xs


def kernel(x_prompt, x_sample, mem_prompt, cache_win_k1, cache_win_v1, cache_win_k2, cache_win_v2, cache_win_k3, cache_win_v3, state_mlstm_C, state_mlstm_n, state_mlstm_m, state_conv, cache_mem_k, cache_mem_v, rel_bias, w_in, mlstm_gate_bias, mlstm_norm_g, conv_w, conv_b, conv_ln_g, conv_ln_b, w_out, xa_wq, xa_wk, xa_wv, xa_wo, ffn_w1, ffn_w2, ln_g, ln_b):
    B, S, D = x_prompt.shape
    BS = x_sample.shape[0]
    MS = SAMPLE_ROWS
    L = math.gcd(S, M_CHUNK)
    bias_c, bias_p, bias_dec = _bias_tables(rel_bias)
    cache_win = (cache_win_k1, cache_win_v1, cache_win_k2, cache_win_v2, cache_win_k3, cache_win_v3)
    sample_state = (state_mlstm_C.astype(F32), state_mlstm_n.astype(F32), state_mlstm_m.astype(F32))
    smem_k = cache_mem_k.reshape(DEPTH, BS, N_MEM, D)
    smem_v = cache_mem_v.reshape(DEPTH, BS, N_MEM, D)

    xp32 = x_prompt.reshape(B * S, D)
    xp16 = xp32.astype(BF16)
    xs32 = jnp.pad(x_sample.reshape(BS, D), ((0, MS - BS), (0, 0)))
    xs16 = xs32.astype(BF16)
    mem16 = mem_prompt.reshape(B * N_MEM, D).astype(BF16)

    p_win = [[] for _ in range(6)]
    new_rows = [[] for _ in range(6)]
    p_C, p_n, p_m, p_conv, p_mk, p_mv = [], [], [], [], [], []
    s_C, s_n, s_m, s_conv = [], [], [], []
    w = _prep_weights(w_in, w_out, xa_wq, xa_wk, xa_wv, xa_wo, ffn_w1, ffn_w2)
    mm = functools.partial(matmul, tm=1024, tn=1024, tk=D)
    xp, xs = (xp32, xp16), (xs32, xs16)
    for l in range(DEPTH):
        gate_b = mlstm_gate_bias[l].astype(F32)
        main, smain = mm(xp[1], w["main"], l, rider=xs[1])
        att, satt = mm(xp[1], w["att"], l, head_major=True, rider=xs[1])
        gmat, sgm = mm(xp[1], w["gate"], l, rider=xs[1])
        satt = satt[:BS]

        mix, C1, n1, m1 = mlstm(main, _gate_layout(gmat, B, S, L), gate_b, mlstm_norm_g[l], None, 0,
                                B=B, S=S, L=L, nvalid=L, out_cols=D)
        state = None
        for g, (window, dil) in enumerate(A_BRANCHES):
            last = g == len(A_BRANCHES) - 1
            out = attn_prompt_branch(att, bias_c[g], bias_p[g], state, mix if last else None,
                                     B=B, S=S, g=g, dil=dil, hb=A_HEADS // 2 if last else A_HEADS)
            if last:
                mix = out
            else:
                state = out
            keep = min(window, S)
            for j in range(2):
                h0 = (3 * g + 1 + j) * A_HEADS
                kv = att[h0:h0 + A_HEADS].reshape(A_HEADS, B, S, A_DH)[:, :, S - keep:]
                p_win[2 * g + j].append(kv.transpose(1, 2, 0, 3))
        mix, conv_tail = conv_block(main, jnp.zeros((B, C_WIDTH - 1, C_CH), F32), conv_w[l], conv_b[l],
                                    conv_ln_g[l], conv_ln_b[l], mix, B=B, S=S, TS=128, nvalid=128, vcol=CONV_COL)
        mk = mm(mem16, w["wk"], l)
        mv = mm(mem16, w["wv"], l)

        LS = M_CHUNK
        main_pad = jnp.zeros((BS, LS, N_MAIN), F32).at[:, 0].set(smain[:BS, :N_MAIN]).reshape(BS * LS, N_MAIN)
        gate_pad = jnp.zeros((BS, LS, GATE_PAD), F32).at[:, 0].set(sgm[:BS]).reshape(BS * LS, GATE_PAD)
        shm, Cs, ns, ms = mlstm(main_pad, _gate_layout(gate_pad, BS, LS, LS), gate_b, mlstm_norm_g[l],
                                sample_state, l, B=BS, S=LS, L=LS, nvalid=1, out_cols=M_HEADS * M_DV)
        shm = shm.reshape(BS, LS, M_HEADS * M_DV)[:, 0]
        sha = attn_decode(satt, cache_win, l, bias_dec)
        for g in range(len(A_BRANCHES)):
            for j in range(2):
                c0 = (3 * g + 1 + j) * COL_BLK
                new_rows[2 * g + j].append(satt[:, c0:c0 + COL_BLK].reshape(BS, 1, A_HEADS, A_DH))
        CT = 8
        cpad = jnp.zeros((BS, CT, 2 * C_CH), F32).at[:, 0].set(smain[:BS, N_MAIN:]).reshape(BS * CT, 2 * C_CH)
        syc, sconv_tail = conv_block(cpad, state_conv[l], conv_w[l], conv_b[l], conv_ln_g[l], conv_ln_b[l], None,
                                     B=BS, S=CT, TS=CT, nvalid=1, vcol=0)
        syc = syc.reshape(BS, CT, C_CH)[:, 0]
        smix = jnp.concatenate([shm, sha.astype(BF16), syc], axis=1)
        smix = jnp.pad(smix, ((0, MS - BS), (0, 0)))

        xp, xs = _trunk_tail(xp, xs, mix, smix, mk.reshape(1, B, N_MEM, D), mv.reshape(1, B, N_MEM, D),
                             smem_k, smem_v, w, l, ln_g[l], ln_b[l], B=B, S=S, BS=BS)
        p_C.append(C1); p_n.append(n1.reshape(B, M_HEADS, M_DK)); p_m.append(m1.reshape(B, M_HEADS))
        p_conv.append(conv_tail)
        p_mk.append(mk.reshape(B, N_MEM, XA_HEADS, XA_DH)); p_mv.append(mv.reshape(B, N_MEM, XA_HEADS, XA_DH))
        s_C.append(Cs); s_n.append(ns.reshape(BS, M_HEADS, M_DK)); s_m.append(ms.reshape(BS, M_HEADS))
        s_conv.append(sconv_tail)
    xp32, xs32 = xp[0], xs[0]

    st = lambda xs: jnp.stack(xs)
    s_win = [shift_append(cache_win[i], st(new_rows[i])) for i in range(6)]
    return (xp32.reshape(B, S, D), xs32[:BS].reshape(BS, 1, D),
            st(p_win[0]), st(p_win[1]), st(p_win[2]), st(p_win[3]), st(p_win[4]), st(p_win[5]),
            st(p_C), st(p_n), st(p_m), st(p_conv), st(p_mk), st(p_mv),
            s_win[0], s_win[1], s_win[2], s_win[3], s_win[4], s_win[5],
            st(s_C), st(s_n), st(s_m), st(s_conv))
```

```python
import functools
import math

import jax
import jax.numpy as jnp
from jax import lax
from jax.experimental import pallas as pl
from jax.experimental.pallas import tpu as pltpu

F32 = jnp.float32
BF16 = jnp.bfloat16

D_MODEL = 4096
DEPTH = 2
M_HEADS, M_DK, M_DV, M_CHUNK = 4, 256, 512, 64
A_HEADS, A_DH = 8, 128
A_BRANCHES = ((128, 1), (512, 4), (2048, 16))
A_J = 128
C_CH, C_WIDTH = 1024, 31
N_BUCKETS, REL_MAX_DIST = 32, 2048
XA_HEADS = 4
XA_DH = D_MODEL // XA_HEADS
N_MEM = 256
D_FF = 4 * D_MODEL
DEEPNORM_ALPHA = (2 * DEPTH) ** 0.25
LN_EPS = 1e-5

N_MAIN = 2 * M_HEADS * M_DK + 2 * M_HEADS * M_DV
N_GATE = 2 * M_HEADS
N_ATT = 9 * A_HEADS * A_DH
COL_BLK = 1024
CONV_COL = N_MAIN // COL_BLK
MIX_ATT_COL = M_HEADS * M_DV // COL_BLK
MIX_CONV_COL = MIX_ATT_COL + 1
GATE_PAD = 128
CONV_HALO = 32
LN_ROWS = 16
SAMPLE_ROWS = 16
COPY_ROWS = 64

VMEM_LIMIT = 56 * 1024 * 1024
NEG_INF = float("-inf")


def _cparams(sem):
    return pltpu.CompilerParams(dimension_semantics=sem, vmem_limit_bytes=VMEM_LIMIT)


def _layer_norm_rows(z, g, b):
    mu = jnp.mean(z, axis=-1, keepdims=True)
    zc = z - mu
    var = jnp.mean(zc * zc, axis=-1, keepdims=True)
    return zc * lax.rsqrt(var + LN_EPS) * g + b


def _mm_kernel(*refs, nk, epilogue, scale, head_major, has_rider):
    if has_rider:
        x_ref, r_ref, w_ref, o_ref, ro_ref = refs[:5]
        scratch = refs[5:]
    else:
        x_ref, w_ref, o_ref = refs[:3]
        scratch = refs[3:]

    def finish(acc, out_ref, as_head_major):
        if epilogue == "relu2":
            r = jnp.maximum(acc, 0.0)
            acc = r * r
        elif scale != 1.0:
            acc = acc * scale
        if as_head_major:
            for c in range(out_ref.shape[0]):
                out_ref[c] = acc[:, c * 128:(c + 1) * 128].astype(out_ref.dtype)
        else:
            out_ref[...] = acc.astype(out_ref.dtype)

    def accumulate(lhs_ref, out_ref, acc_ref, as_head_major):
        def prod():
            if len(lhs_ref.shape) == 2:
                return jnp.dot(lhs_ref[...].astype(BF16), w_ref[...], preferred_element_type=F32)
            cb = lhs_ref.shape[2]
            out = None
            for c in range(lhs_ref.shape[0]):
                part = jnp.dot(lhs_ref[c].astype(BF16), w_ref[c * cb:(c + 1) * cb, :], preferred_element_type=F32)
                out = part if out is None else out + part
            return out

        if nk == 1:
            finish(prod(), out_ref, as_head_major)
            return
        k = pl.program_id(2)

        @pl.when(k == 0)
        def _():
            acc_ref[...] = prod()

        @pl.when(k > 0)
        def _():
            acc_ref[...] += prod()

        @pl.when(k == nk - 1)
        def _():
            finish(acc_ref[...], out_ref, as_head_major)

    accumulate(x_ref, o_ref, scratch[0] if nk > 1 else None, head_major)
    if has_rider:
        @pl.when(pl.program_id(1) == 0)
        def _():
            accumulate(r_ref, ro_ref, scratch[1] if nk > 1 else None, False)


def matmul(x, w, layer, *, tm, tn, tk, out_dtype=F32, epilogue="none", scale=1.0, out_layout="rows",
           lhs_slabs=False, rider=None):
    if lhs_slabs:
        nslab, M, cb = x.shape
        K = nslab * cb
    else:
        M, K = x.shape
    N = w.shape[2]
    tm, tn, tk = min(tm, M), min(tn, N), min(tk, K)
    assert M % tm == 0 and N % tn == 0 and K % tk == 0, (x.shape, w.shape, tm, tn, tk)
    nk = K // tk
    head_major = out_layout == "heads"
    if head_major:
        out_specs = pl.BlockSpec((tn // 128, tm, 128), lambda j, i, k: (j, i, 0))
        out_shape = jax.ShapeDtypeStruct((N // 128, M, 128), out_dtype)
    elif out_layout == "slabs":
        out_specs = pl.BlockSpec((None, tm, tn), lambda j, i, k: (j, i, 0))
        out_shape = jax.ShapeDtypeStruct((N // tn, M, tn), out_dtype)
    else:
        out_specs = pl.BlockSpec((tm, tn), lambda j, i, k: (i, j))
        out_shape = jax.ShapeDtypeStruct((M, N), out_dtype)
    if lhs_slabs:
        assert tk % cb == 0
        x_spec = pl.BlockSpec((tk // cb, tm, cb), lambda j, i, k: (k, i, 0))
    else:
        x_spec = pl.BlockSpec((tm, tk), lambda j, i, k: (i, k))
    w_spec = pl.BlockSpec((None, tk, tn), lambda j, i, k: (layer, k, j))
    scratch = [pltpu.VMEM((tm, tn), F32)] if nk > 1 else []
    if rider is None:
        in_specs, args = [x_spec, w_spec], (x, w)
    else:
        R = rider.shape[0]
        in_specs, args = [x_spec, pl.BlockSpec((R, tk), lambda j, i, k: (0, k)), w_spec], (x, rider, w)
        out_specs = (out_specs, pl.BlockSpec((R, tn), lambda j, i, k: (0, j)))
        out_shape = (out_shape, jax.ShapeDtypeStruct((R, N), out_dtype))
        scratch += [pltpu.VMEM((R, tn), F32)] if nk > 1 else []
    return pl.pallas_call(
        functools.partial(_mm_kernel, nk=nk, epilogue=epilogue, scale=scale, head_major=head_major,
                          has_rider=rider is not None),
        grid=(N // tn, M // tm, nk),
        in_specs=in_specs, out_specs=out_specs, out_shape=out_shape, scratch_shapes=scratch,
        compiler_params=_cparams(("parallel", "arbitrary", "arbitrary")),
        name=f"mm_{epilogue}",
    )(*args)


def _post_ln_kernel(res_ref, sub_ref, g_ref, b_ref, o32_ref, o16_ref):
    def rows(i, carry):
        r = pl.ds(pl.multiple_of(i * LN_ROWS, LN_ROWS), LN_ROWS)
        sub = jnp.concatenate([sub_ref[c, r, :] for c in range(sub_ref.shape[0])], axis=1)
        y = _layer_norm_rows(DEEPNORM_ALPHA * res_ref[r, :] + sub, g_ref[...], b_ref[...])
        o32_ref[r, :] = y
        o16_ref[r, :] = y.astype(BF16)
        return carry
    lax.fori_loop(0, o32_ref.shape[0] // LN_ROWS, rows, 0, unroll=2)


def post_ln(res, sub, g, b, *, tr):
    M, N = res.shape
    tr = min(tr, M)
    assert M % tr == 0 and tr % (2 * LN_ROWS) == 0 or tr == LN_ROWS
    row = pl.BlockSpec((tr, N), lambda i: (i, 0))
    vec = pl.BlockSpec((1, N), lambda i: (0, 0))
    slabs = pl.BlockSpec((sub.shape[0], tr, sub.shape[2]), lambda i: (0, i, 0))
    return pl.pallas_call(
        _post_ln_kernel, grid=(M // tr,), in_specs=[row, slabs, vec, vec], out_specs=(row, row),
        out_shape=(jax.ShapeDtypeStruct((M, N), F32), jax.ShapeDtypeStruct((M, N), BF16)),
        compiler_params=_cparams(("parallel",)),
        name="post_ln",
    )(res, sub, g.reshape(1, N).astype(F32), b.reshape(1, N).astype(F32))


def _mlstm_kernel(gb_ref, g_ref, q_ref, k_ref, v_ref, o_ref, ng_ref, c0_ref, n0_ref, m0_ref,
                  h_ref, c_ref, n_ref, m_ref, *, L, nvalid):
    c = pl.program_id(1)

    @pl.when(c == 0)
    def _():
        c_ref[...] = c0_ref[...]
        n_ref[...] = n0_ref[...]
        m_ref[...] = m0_ref[...]

    for hd in range(M_HEADS):
        _mlstm_head(hd, c, gb_ref, g_ref, q_ref, k_ref, v_ref, o_ref, ng_ref, h_ref, c_ref, n_ref, m_ref,
                    L=L, nvalid=nvalid)
    if h_ref.shape[1] > M_HEADS * M_DV:
        h_ref[:, M_HEADS * M_DV:] = jnp.zeros((L, h_ref.shape[1] - M_HEADS * M_DV), h_ref.dtype)


def _mlstm_head(hd, c, gb_ref, g_ref, q_ref, k_ref, v_ref, o_ref, ng_ref, h_ref, c_ref, n_ref, m_ref, *, L, nvalid):
    dk, dv = M_DK, M_DV
    ks, vs = slice(hd * dk, (hd + 1) * dk), slice(hd * dv, (hd + 1) * dv)
    ig = g_ref[0, 0, hd, pl.ds(c, 1), :] + gb_ref[0, hd]
    fg = g_ref[0, 1, hd, pl.ds(c, 1), :] + gb_ref[1, hd]
    lf = -(jnp.maximum(-fg, 0.0) + jnp.log1p(jnp.exp(-jnp.abs(fg))))
    if nvalid < L:
        lane = lax.broadcasted_iota(jnp.int32, (1, L), 1)
        ig = jnp.where(lane < nvalid, ig, NEG_INF)
        lf = jnp.where(lane < nvalid, lf, 0.0)
    qi = lax.broadcasted_iota(jnp.int32, (L, L), 0)
    si = lax.broadcasted_iota(jnp.int32, (L, L), 1)
    eye = qi == si
    tri = si <= qi
    lf_col = jnp.sum(jnp.where(eye, lf, 0.0), axis=1, keepdims=True)
    b_col = jnp.sum(jnp.where(tri, lf, 0.0), axis=1, keepdims=True)
    b_row = jnp.sum(jnp.where(qi <= si, lf_col, 0.0), axis=0, keepdims=True)
    m_prev = m_ref[0, hd]
    dmat = jnp.where(tri, b_col - b_row + ig, NEG_INF)
    inter = b_col + m_prev
    mt = jnp.maximum(inter, jnp.max(dmat, axis=1, keepdims=True))
    w_intra = jnp.exp(dmat - mt)
    w_inter = jnp.exp(inter - mt)

    q = q_ref[:, ks]
    kf = k_ref[:, ks] * (M_DK ** -0.5)
    per_slab = COL_BLK // dv
    vslab = lambda ref: ref[hd // per_slab, :, (hd % per_slab) * dv:(hd % per_slab + 1) * dv]
    qb, kb, vb = q.astype(BF16), kf.astype(BF16), vslab(v_ref).astype(BF16)
    c_prev = c_ref[0, hd]
    n_prev = n_ref[0, hd]
    sc = lax.dot_general(qb, kb, (((1,), (1,)), ((), ())), preferred_element_type=F32) * w_intra
    num = (w_inter * jnp.dot(qb, c_prev.astype(BF16), preferred_element_type=F32)
           + jnp.dot(sc.astype(BF16), vb, preferred_element_type=F32))
    den = w_inter * jnp.sum(q * n_prev, axis=1, keepdims=True) + jnp.sum(sc, axis=1, keepdims=True)
    h = num / jnp.maximum(jnp.abs(den), jnp.exp(-mt))

    m_new = mt[L - 1:L, :]
    b_last = b_col[L - 1:L, :]
    w_end = jnp.exp(b_last - b_row + ig - m_new)
    w_end_col = jnp.sum(jnp.where(eye, w_end, 0.0), axis=1, keepdims=True)
    decay = jnp.exp(b_last + m_prev - m_new)
    kw = kf * w_end_col
    c_ref[0, hd] = decay * c_prev + lax.dot_general(
        kw.astype(BF16), vb, (((0,), (0,)), ((), ())), preferred_element_type=F32)
    n_ref[0, hd] = decay * n_prev + jnp.sum(kw, axis=0, keepdims=True)
    m_ref[0, hd] = m_new

    hn = h * lax.rsqrt(jnp.mean(h * h, axis=-1, keepdims=True) + LN_EPS) * ng_ref[:, vs]
    h_ref[:, vs] = (jax.nn.sigmoid(vslab(o_ref)) * hn).astype(h_ref.dtype)


def mlstm(proj, gates, gate_b, norm_g, state, layer, *, B, S, L, nvalid, out_cols):
    nc = S // L
    H, dk, dv = M_HEADS, M_DK, M_DV
    if state is None:
        c0 = jnp.zeros((1, B, H, dk, dv), F32)
        n0 = jnp.zeros((1, B, H, 1, dk), F32)
        m0 = jnp.zeros((1, B, H, 1, 1), F32)
        layer = 0
    else:
        c0 = state[0]
        n0 = state[1].reshape(state[1].shape[0], B, H, 1, dk)
        m0 = state[2].reshape(state[2].shape[0], B, H, 1, 1)
    st_in = lambda b, c: (layer, b, 0, 0, 0)
    st4 = lambda b, c: (b, 0, 0, 0)
    row = lambda b, c: b * nc + c
    assert H * dk == COL_BLK and H * dv == 2 * COL_BLK
    return pl.pallas_call(
        functools.partial(_mlstm_kernel, L=L, nvalid=nvalid),
        grid=(B, nc),
        in_specs=[
            pl.BlockSpec(memory_space=pltpu.SMEM),
            pl.BlockSpec((1, 2, H, nc, L), lambda b, c: (b, 0, 0, 0, 0)),
            pl.BlockSpec((None, L, COL_BLK), lambda b, c: (0, row(b, c), 0)),
            pl.BlockSpec((None, L, COL_BLK), lambda b, c: (1, row(b, c), 0)),
            pl.BlockSpec((2, L, COL_BLK), lambda b, c: (1, row(b, c), 0)),
            pl.BlockSpec((2, L, COL_BLK), lambda b, c: (2, row(b, c), 0)),
            pl.BlockSpec((1, H * dv), lambda b, c: (0, 0)),
            pl.BlockSpec((None, 1, H, dk, dv), st_in),
            pl.BlockSpec((None, 1, H, 1, dk), st_in),
            pl.BlockSpec((None, 1, H, 1, 1), st_in),
        ],
        out_specs=(
            pl.BlockSpec((L, out_cols), lambda b, c: (row(b, c), 0)),
            pl.BlockSpec((1, H, dk, dv), st4),
            pl.BlockSpec((1, H, 1, dk), st4),
            pl.BlockSpec((1, H, 1, 1), st4),
        ),
        out_shape=(
            jax.ShapeDtypeStruct((B * S, out_cols), BF16),
            jax.ShapeDtypeStruct((B, H, dk, dv), F32),
            jax.ShapeDtypeStruct((B, H, 1, dk), F32),
            jax.ShapeDtypeStruct((B, H, 1, 1), F32),
        ),
        compiler_params=_cparams(("parallel", "arbitrary")),
        name="mlstm",
    )(gate_b, gates, proj, proj, proj, proj, norm_g.reshape(1, H * dv), c0, n0, m0)


def _attn_prompt_kernel(*refs, dil, heads, head0, first, last, with_prev):
    bias_ref, q_ref, kc_ref, vc_ref = refs[:4]
    refs = refs[4:]
    if with_prev:
        kp_ref, vp_ref = refs[:2]
        refs = refs[2:]
    if not first:
        num_in_ref, sm_in_ref = refs[:2]
        refs = refs[2:]
    if last:
        _, out_ref, o32_ref = refs
    else:
        num_out_ref, sm_out_ref = refs
    T = A_J
    lane = lax.broadcasted_iota(jnp.int32, (T, 128), 1)
    if with_prev:
        key_ok = (pl.program_id(2) > 0) | (lax.broadcasted_iota(jnp.int32, (T, 2 * T), 1) >= T)
    nt = (((1,), (1,)), ((), ()))
    for r in range(dil):
        rows = pl.ds(r, T, stride=dil) if dil > 1 else pl.ds(0, T)
        sm_old = None if first else sm_in_ref[rows, :]
        sm_new = jnp.zeros((T, 128), F32)
        for hl in range(heads):
            hg = head0 + hl
            qh = (q_ref[hl, rows, :] * (A_DH ** -0.5)).astype(BF16)
            if with_prev:
                kk = jnp.concatenate([kp_ref[hl, rows, :], kc_ref[hl, rows, :]], axis=0).astype(BF16)
                vv = jnp.concatenate([vp_ref[hl, rows, :], vc_ref[hl, rows, :]], axis=0).astype(BF16)
                logits = lax.dot_general(qh, kk, nt, preferred_element_type=F32) + bias_ref[hg]
                logits = jnp.where(key_ok, logits, NEG_INF)
            else:
                kk = kc_ref[hl, rows, :].astype(BF16)
                vv = vc_ref[hl, rows, :].astype(BF16)
                logits = lax.dot_general(qh, kk, nt, preferred_element_type=F32) + bias_ref[hg]
            mx = jnp.max(logits, axis=1, keepdims=True)
            p = jnp.exp(logits - mx)
            s = jnp.sum(p, axis=1, keepdims=True)
            o = jnp.dot(p.astype(BF16), vv, preferred_element_type=F32)
            if not first:
                s_old = sm_old[:, hg:hg + 1]
                m_old = sm_old[:, A_HEADS + hg:A_HEADS + hg + 1]
                m_all = jnp.maximum(m_old, mx)
                a_old = jnp.exp(m_old - m_all)
                a_new = jnp.exp(mx - m_all)
                o = a_old * num_in_ref[hl, rows, :] + a_new * o
                s = a_old * s_old + a_new * s
                mx = m_all
            if last:
                o32_ref[hl, rows, :] = o / s
            else:
                num_out_ref[hl, rows, :] = o
                sm_new = jnp.where(lane == hg, s, sm_new)
                sm_new = jnp.where(lane == A_HEADS + hg, mx, sm_new)
        if not last:
            sm_out_ref[rows, :] = sm_new
    if last:
        for hl in range(heads):
            out_ref[:, hl * A_DH:(hl + 1) * A_DH] = o32_ref[hl].astype(out_ref.dtype)


def attn_prompt_branch(att, bias_c, bias_p, state, mix, *, B, S, g, dil, hb):
    first, last = state is None, mix is not None
    span = A_J * dil
    nb = S // span
    with_prev = nb > 1
    nhb = A_HEADS // hb
    assert nhb == 1 or last
    cur = lambda part: pl.BlockSpec((hb, span, A_DH), lambda b, c, n: ((3 * g + part) * nhb + c, b * nb + n, 0))
    prev = lambda part: pl.BlockSpec(
        (hb, span, A_DH), lambda b, c, n: ((3 * g + part) * nhb + c, jnp.maximum(b * nb + n - 1, 0), 0))
    bias = jnp.concatenate([bias_p, bias_c], axis=2) if with_prev else bias_c
    num_spec = pl.BlockSpec((hb, span, A_DH), lambda b, c, n: (c, b * nb + n, 0))
    sm_spec = pl.BlockSpec((span, 128), lambda b, c, n: (b * nb + n, 0))
    in_specs = [pl.BlockSpec(bias.shape, lambda b, c, n: (0, 0, 0)), cur(0), cur(1), cur(2)]
    args = [bias, att, att, att]
    if with_prev:
        in_specs += [prev(1), prev(2)]
        args += [att, att]
    if not first:
        in_specs += [num_spec, sm_spec]
        args += list(state)
    aliases = {}
    if last:
        in_specs.append(pl.BlockSpec(memory_space=pl.ANY))
        args.append(mix)
        aliases = {len(args) - 1: 0}
        out_specs = pl.BlockSpec((span, hb * A_DH), lambda b, c, n: (b * nb + n, MIX_ATT_COL * nhb + c))
        out_shape = jax.ShapeDtypeStruct(mix.shape, mix.dtype)
    else:
        out_specs = (num_spec, sm_spec)
        out_shape = (jax.ShapeDtypeStruct((A_HEADS, B * S, A_DH), F32), jax.ShapeDtypeStruct((B * S, 128), F32))

    def body(*refs):
        kw = dict(dil=dil, heads=hb, first=first, last=last, with_prev=with_prev)
        if nhb == 1:
            _attn_prompt_kernel(*refs, head0=0, **kw)
        else:
            for cb in range(nhb):
                @pl.when(pl.program_id(1) == cb)
                def _(cb=cb):
                    _attn_prompt_kernel(*refs, head0=cb * hb, **kw)

    return pl.pallas_call(
        body, grid=(B, nhb, nb), in_specs=in_specs, out_specs=out_specs, out_shape=out_shape,
        input_output_aliases=aliases,
        scratch_shapes=[pltpu.VMEM((hb, span, A_DH), F32)] if last else [],
        compiler_params=_cparams(("parallel", "parallel", "arbitrary")),
        name=f"attn_prompt_{g}",
    )(*args)


def _attn_decode_kernel(bias_ref, a_ref, k1_ref, v1_ref, k2_ref, v2_ref, k3_ref, v3_ref, out_ref):
    caches = ((k1_ref, v1_ref), (k2_ref, v2_ref), (k3_ref, v3_ref))
    nt = (((1,), (1,)), ((), ()))
    hw = A_HEADS * A_DH
    for h in range(A_HEADS):
        outs, dens, maxs = [], [], []
        for g in range(3):
            base = 3 * g * hw + h * A_DH
            qf = (a_ref[0, :, base:base + A_DH] * (A_DH ** -0.5)).astype(BF16)
            kn = a_ref[0, :, base + hw:base + hw + A_DH].astype(BF16).astype(F32)
            vn = a_ref[0, :, base + 2 * hw:base + 2 * hw + A_DH].astype(BF16).astype(F32)
            kc = caches[g][0][:, h, :].astype(BF16)
            vc = caches[g][1][:, h, :].astype(BF16)
            q16 = jnp.broadcast_to(qf, (16, A_DH))
            lc = lax.dot_general(q16, kc, nt, preferred_element_type=F32)[0:1, :] + bias_ref[g, h:h + 1, 0:A_J]
            ln = (jnp.sum(qf.astype(F32) * kn, axis=1, keepdims=True)
                  + bias_ref[g, h:h + 1, A_J:A_J + 1])
            mx = jnp.maximum(jnp.max(lc, axis=1, keepdims=True), ln)
            pc = jnp.exp(lc - mx)
            pn = jnp.exp(ln - mx)
            p16 = jnp.broadcast_to(pc.astype(BF16), (16, A_J))
            o = jnp.dot(p16, vc, preferred_element_type=F32)[0:1, :] + pn.astype(BF16).astype(F32) * vn
            outs.append(o)
            dens.append(jnp.sum(pc, axis=1, keepdims=True) + pn)
            maxs.append(mx)
        m_all = jnp.maximum(jnp.maximum(maxs[0], maxs[1]), maxs[2])
        wts = [jnp.exp(m - m_all) for m in maxs]
        num = wts[0] * outs[0] + wts[1] * outs[1] + wts[2] * outs[2]
        den = wts[0] * dens[0] + wts[1] * dens[1] + wts[2] * dens[2]
        out_ref[0, :, h * A_DH:(h + 1) * A_DH] = num / den


def attn_decode(att, caches, layer, bias_dec):
    B = att.shape[0]
    hw = A_HEADS * A_DH
    in_specs = [pl.BlockSpec(bias_dec.shape, lambda b: (0, 0, 0)),
                pl.BlockSpec((1, 1, N_ATT), lambda b: (b, 0, 0))]
    args = [bias_dec, att.reshape(B, 1, N_ATT)]
    for g, (window, dil) in enumerate(A_BRANCHES):
        for t in caches[2 * g:2 * g + 2]:
            assert t.shape[2] == window
            in_specs.append(pl.BlockSpec((None, None, A_J, None, A_HEADS, A_DH), lambda b: (layer, b, 0, 0, 0, 0)))
            args.append(t.reshape(t.shape[0], B, window // dil, dil, A_HEADS, A_DH))
    out = pl.pallas_call(
        _attn_decode_kernel, grid=(B,), in_specs=in_specs,
        out_specs=pl.BlockSpec((1, 1, hw), lambda b: (b, 0, 0)),
        out_shape=jax.ShapeDtypeStruct((B, 1, hw), F32),
        compiler_params=_cparams(("parallel",)),
        name="attn_decode",
    )(*args)
    return out.reshape(B, hw)


def _shift_kernel(c_ref, new_ref, o_ref):
    P = c_ref.shape[0]
    R = min(COPY_ROWS, P)

    def chunk(i, carry):
        o_ref[pl.ds(i * R, R)] = c_ref[pl.ds(i * R + 1, R)]
        return carry
    lax.fori_loop(0, P // R - 1, chunk, 0)
    o_ref[P - R:P - 1] = c_ref[P - R + 1:P]
    o_ref[P - 1:P] = new_ref[...]


def shift_append(cache, new_rows):
    Dp, B, P, H, Dh = cache.shape
    blk = lambda rows: pl.BlockSpec((None, None, rows, H, Dh), lambda l, b: (l, b, 0, 0, 0))
    return pl.pallas_call(
        _shift_kernel, grid=(Dp, B), in_specs=[blk(P), blk(1)], out_specs=blk(P),
        out_shape=jax.ShapeDtypeStruct(cache.shape, cache.dtype),
        compiler_params=_cparams(("parallel", "parallel")),
        name="shift_append",
    )(cache, new_rows.astype(cache.dtype))


def _conv_kernel(cv_ref, cg_ref, hv_ref, hg_ref, past_ref, cw_ref, cb_ref, lg_ref, lb_ref, *rest,
                 TS, RC, nvalid):
    y_ref, tail_ref, xs_ref = rest[-3:]
    t = pl.program_id(1)
    H = CONV_HALO

    @pl.when(t == 0)
    def _():
        xs_ref[0:H, :] = past_ref[0]

    @pl.when(t > 0)
    def _():
        xs_ref[0:H, :] = hv_ref[...] * jax.nn.sigmoid(hg_ref[...])

    xs_ref[H:H + TS, :] = cv_ref[...] * jax.nn.sigmoid(cg_ref[...])
    lead = H - (C_WIDTH - 1)
    for rc in range(TS // RC):
        acc = jnp.zeros((RC, C_CH), F32)
        for w in range(C_WIDTH):
            r0 = rc * RC + lead + w
            acc = acc + xs_ref[r0:r0 + RC, :] * cw_ref[w:w + 1, :]
        y = _layer_norm_rows(acc + cb_ref[...], lg_ref[...], lb_ref[...])
        y_ref[rc * RC:(rc + 1) * RC, :] = (y * jax.nn.sigmoid(y)).astype(y_ref.dtype)
    tail_ref[0] = xs_ref[nvalid:nvalid + H, :]


def conv_block(proj, past, conv_w, conv_b, ln_g, ln_b, mix, *, B, S, TS, nvalid, vcol):
    nt = S // TS
    H = CONV_HALO
    RC = min(TS, 32)
    past_p = jnp.pad(past.astype(F32), ((0, 0), (H - (C_WIDTH - 1), 0), (0, 0)))
    cur = lambda col: pl.BlockSpec((None, TS, C_CH), lambda b, t: (col, b * nt + t, 0))
    halo_rows = min(H, B * S)
    halo = lambda col: pl.BlockSpec(
        (None, halo_rows, C_CH), lambda b, t: (col, jnp.maximum((b * S + t * TS) // H - 1, 0), 0))
    vec = pl.BlockSpec((1, C_CH), lambda b, t: (0, 0))
    in_specs = [cur(vcol), cur(vcol + 1), halo(vcol), halo(vcol + 1),
                pl.BlockSpec((1, H, C_CH), lambda b, t: (b, 0, 0)),
                pl.BlockSpec((H, C_CH), lambda b, t: (0, 0)), vec, vec, vec]
    args = [proj, proj, proj, proj, past_p, jnp.pad(conv_w.astype(F32), ((0, H - C_WIDTH), (0, 0))),
            conv_b.reshape(1, C_CH), ln_g.reshape(1, C_CH), ln_b.reshape(1, C_CH)]
    aliases = {}
    if mix is None:
        y_spec = pl.BlockSpec((TS, C_CH), lambda b, t: (b * nt + t, 0))
        y_shape = jax.ShapeDtypeStruct((B * S, C_CH), BF16)
    else:
        in_specs.append(pl.BlockSpec(memory_space=pl.ANY))
        args.append(mix)
        aliases = {len(args) - 1: 0}
        y_spec = pl.BlockSpec((TS, C_CH), lambda b, t: (b * nt + t, MIX_CONV_COL))
        y_shape = jax.ShapeDtypeStruct(mix.shape, mix.dtype)
    y, tail = pl.pallas_call(
        functools.partial(_conv_kernel, TS=TS, RC=RC, nvalid=nvalid),
        grid=(B, nt), in_specs=in_specs,
        out_specs=(y_spec, pl.BlockSpec((1, H, C_CH), lambda b, t: (b, 0, 0))),
        out_shape=(y_shape, jax.ShapeDtypeStruct((B, H, C_CH), F32)),
        scratch_shapes=[pltpu.VMEM((H + TS, C_CH), F32)],
        input_output_aliases=aliases,
        compiler_params=_cparams(("parallel", "arbitrary")),
        name="conv_ln_silu",
    )(*args)
    return y, tail[:, H - (C_WIDTH - 1):]


def _xattn_kernel(q_ref, mk_ref, mv_ref, o_ref):
    q = q_ref[...]
    logits = lax.dot_general(q, mk_ref[...].astype(BF16), (((1,), (1,)), ((), ())),
                             preferred_element_type=F32)
    e = jnp.exp(logits - jnp.max(logits, axis=-1, keepdims=True))
    p = e / jnp.sum(e, axis=-1, keepdims=True)
    o_ref[0] = jnp.dot(p.astype(BF16), mv_ref[...].astype(BF16), preferred_element_type=F32).astype(o_ref.dtype)


def cross_attn(q, mk, mv, layer, *, B, S, TS):
    TS = min(TS, S)
    nt = S // TS
    mem = pl.BlockSpec((None, None, N_MEM, XA_DH), lambda b, h, t: (layer, b, 0, h))
    return pl.pallas_call(
        _xattn_kernel, grid=(B, XA_HEADS, nt),
        in_specs=[pl.BlockSpec((None, TS, XA_DH), lambda b, h, t: (h, b * nt + t, 0)), mem, mem],
        out_specs=pl.BlockSpec((1, TS, XA_DH), lambda b, h, t: (b, t, h)),
        out_shape=jax.ShapeDtypeStruct((B, S, XA_HEADS * XA_DH), BF16),
        compiler_params=_cparams(("parallel", "parallel", "arbitrary")),
        name="cross_attn",
    )(q, mk, mv)


def _rel_bucket(dist):
    exact = N_BUCKETS // 2
    lg = jnp.log(jnp.maximum(dist, 1).astype(F32) / exact) / math.log(REL_MAX_DIST / exact)
    large = jnp.minimum(exact + (lg * (N_BUCKETS - exact)).astype(jnp.int32), N_BUCKETS - 1)
    return jnp.where(dist < exact, dist, large)


def _bias_tables(rel_bias):
    J = A_J
    cur, prev, dec = [], [], []
    for g, (window, dil) in enumerate(A_BRANCHES):
        table = rel_bias[:, g * A_HEADS:(g + 1) * A_HEADS].astype(F32)
        bucket = _rel_bucket(dil * jnp.arange(window // dil + 1))
        onehot = bucket[:, None] == jnp.arange(N_BUCKETS)[None, :]
        bias_j = jnp.sum(jnp.where(onehot[:, :, None], table[None], 0.0), axis=1)
        rev = bias_j.T[:, ::-1]
        u = jnp.concatenate([rev, jnp.full((A_HEADS, J), NEG_INF, F32)], axis=1)
        tile = jnp.broadcast_to(u[:, None, :], (A_HEADS, J, 2 * J + 1)).reshape(A_HEADS, J * (2 * J + 1))
        tile = tile[:, :J * 2 * J].reshape(A_HEADS, J, 2 * J)
        prev.append(tile[:, :, :J])
        cur.append(tile[:, :, J:])
        dec.append(jnp.pad(rev, ((0, 0), (0, 2 * J - (J + 1)))))
    return cur, prev, jnp.stack(dec)


def _prep_weights(w_in, w_out, xa_wq, xa_wk, xa_wv, xa_wo, ffn_w1, ffn_w2):
    a0 = N_MAIN + N_GATE
    w_gate = jnp.pad(w_in[:, :, N_MAIN:a0], ((0, 0), (0, 0), (0, GATE_PAD - N_GATE))).astype(BF16)
    w_main = jnp.concatenate([w_in[:, :, :N_MAIN], w_in[:, :, a0 + N_ATT:]], axis=2).astype(BF16)
    w_att = w_in[:, :, a0:a0 + N_ATT].astype(BF16)
    cast = lambda w: w.astype(BF16)
    return dict(main=w_main, att=w_att, gate=w_gate,
                out=cast(w_out), wq=cast(xa_wq), wk=cast(xa_wk), wv=cast(xa_wv),
                wo=cast(xa_wo), w1=cast(ffn_w1), w2=cast(ffn_w2))


def _gate_layout(gmat, B, S, L):
    g = gmat[:, :N_GATE].reshape(B, S // L, L, 2, M_HEADS)
    return g.transpose(0, 3, 4, 1, 2)


def _trunk_tail(xp, xs, mix, smix, mk, mv, smk, smv, w, l, lg, lb, *, B, S, BS):
    mm = functools.partial(matmul, tm=1024, tn=1024, tk=D_MODEL)
    MS = xs[0].shape[0]

    def both_ln(sub, ssub, i):
        return (post_ln(xp[0], sub, lg[i], lb[i], tr=256),
                post_ln(xs[0], _to_slabs(ssub), lg[i], lb[i], tr=MS))

    xp, xs = both_ln(*mm(mix, w["out"], l, out_layout="slabs", rider=smix), 0)
    q, sq = mm(xp[1], w["wq"], l, out_dtype=BF16, scale=XA_DH ** -0.5, out_layout="slabs", rider=xs[1])
    o = cross_attn(q, mk, mv, 0, B=B, S=S, TS=512).reshape(B * S, D_MODEL)
    sq = jnp.zeros((BS, MS, D_MODEL), BF16).at[:, 0].set(sq[:BS]).reshape(BS * MS, D_MODEL)
    so = jnp.pad(cross_attn(_to_slabs(sq), smk, smv, l, B=BS, S=MS, TS=MS)[:, 0], ((0, MS - BS), (0, 0)))
    xp, xs = both_ln(*mm(o, w["wo"], l, out_layout="slabs", rider=so), 1)
    hid, shid = mm(xp[1], w["w1"], l, out_dtype=BF16, epilogue="relu2", out_layout="slabs", rider=xs[1])
    xp, xs = both_ln(*mm(hid, w["w2"], l, out_layout="slabs", lhs_slabs=True, rider=shid), 2)
    return xp, xs


def _to_slabs(rows):
    M, N = rows.shape
    return rows.reshape(M, N // COL_BLK, COL_BLK).transpose(1, 0, 2)


def kernel(x_prompt, x_sample, mem_prompt, cache_win_k1, cache_win_v1, cache_win_k2, cache_win_v2, cache_win_k3, cache_win_v3, state_mlstm_C, state_mlstm_n, state_mlstm_m, state_conv, cache_mem_k, cache_mem_v, rel_bias, w_in, mlstm_gate_bias, mlstm_norm_g, conv_w, conv_b, conv_ln_g, conv_ln_b, w_out, xa_wq, xa_wk, xa_wv, xa_wo, ffn_w1, ffn_w2, ln_g, ln_b):
    B, S, D = x_prompt.shape
    BS = x_sample.shape[0]
    MS = SAMPLE_ROWS
    L = math.gcd(S, M_CHUNK)
    bias_c, bias_p, bias_dec = _bias_tables(rel_bias)
    cache_win = (cache_win_k1, cache_win_v1, cache_win_k2, cache_win_v2, cache_win_k3, cache_win_v3)
    sample_state = (state_mlstm_C.astype(F32), state_mlstm_n.astype(F32), state_mlstm_m.astype(F32))
    smem_k = cache_mem_k.reshape(DEPTH, BS, N_MEM, D)
    smem_v = cache_mem_v.reshape(DEPTH, BS, N_MEM, D)

    xp32 = x_prompt.reshape(B * S, D)
    xp16 = xp32.astype(BF16)
    xs32 = jnp.pad(x_sample.reshape(BS, D), ((0, MS - BS), (0, 0)))
    xs16 = xs32.astype(BF16)
    mem16 = mem_prompt.reshape(B * N_MEM, D).astype(BF16)

    p_win = [[] for _ in range(6)]
    new_rows = [[] for _ in range(6)]
    p_C, p_n, p_m, p_conv, p_mk, p_mv = [], [], [], [], [], []
    s_C, s_n, s_m, s_conv = [], [], [], []
    w = _prep_weights(w_in, w_out, xa_wq, xa_wk, xa_wv, xa_wo, ffn_w1, ffn_w2)
    mm = functools.partial(matmul, tm=1024, tn=1024, tk=D)
    xp, xs = (xp32, xp16), (xs32, xs16)
    for l in range(DEPTH):
        gate_b = mlstm_gate_bias[l].astype(F32)
        main, smain = mm(xp[1], w["main"], l, out_layout="slabs", rider=xs[1])
        att, satt = mm(xp[1], w["att"], l, out_layout="heads", rider=xs[1])
        gmat, sgm = mm(xp[1], w["gate"], l, rider=xs[1])
        satt = satt[:BS]

        mix, C1, n1, m1 = mlstm(main, _gate_layout(gmat, B, S, L), gate_b, mlstm_norm_g[l], None, 0,
                                B=B, S=S, L=L, nvalid=L, out_cols=D)
        state = None
        for g, (window, dil) in enumerate(A_BRANCHES):
            last = g == len(A_BRANCHES) - 1
            out = attn_prompt_branch(att, bias_c[g], bias_p[g], state, mix if last else None,
                                     B=B, S=S, g=g, dil=dil, hb=A_HEADS // 2 if last else A_HEADS)
            if last:
                mix = out
            else:
                state = out
            keep = min(window, S)
            for j in range(2):
                h0 = (3 * g + 1 + j) * A_HEADS
                kv = att[h0:h0 + A_HEADS].reshape(A_HEADS, B, S, A_DH)[:, :, S - keep:]
                p_win[2 * g + j].append(kv.transpose(1, 2, 0, 3))
        mix, conv_tail = conv_block(main, jnp.zeros((B, C_WIDTH - 1, C_CH), F32), conv_w[l], conv_b[l],
                                    conv_ln_g[l], conv_ln_b[l], mix, B=B, S=S, TS=128, nvalid=128, vcol=CONV_COL)
        mk = mm(mem16, w["wk"], l)
        mv = mm(mem16, w["wv"], l)

        LS = M_CHUNK
        main_pad = jnp.zeros((BS, LS, N_MAIN), F32).at[:, 0].set(smain[:BS, :N_MAIN]).reshape(BS * LS, N_MAIN)
        gate_pad = jnp.zeros((BS, LS, GATE_PAD), F32).at[:, 0].set(sgm[:BS]).reshape(BS * LS, GATE_PAD)
        shm, Cs, ns, ms = mlstm(_to_slabs(main_pad), _gate_layout(gate_pad, BS, LS, LS), gate_b, mlstm_norm_g[l],
                                sample_state, l, B=BS, S=LS, L=LS, nvalid=1, out_cols=M_HEADS * M_DV)
        shm = shm.reshape(BS, LS, M_HEADS * M_DV)[:, 0]
        sha = attn_decode(satt, cache_win, l, bias_dec)
        for g in range(len(A_BRANCHES)):
            for j in range(2):
                c0 = (3 * g + 1 + j) * COL_BLK
                new_rows[2 * g + j].append(satt[:, c0:c0 + COL_BLK].reshape(BS, 1, A_HEADS, A_DH))
        CT = 8
        cpad = jnp.zeros((BS, CT, 2 * C_CH), F32).at[:, 0].set(smain[:BS, N_MAIN:]).reshape(BS * CT, 2 * C_CH)
        syc, sconv_tail = conv_block(_to_slabs(cpad), state_conv[l], conv_w[l], conv_b[l], conv_ln_g[l], conv_ln_b[l], None,
                                     B=BS, S=CT, TS=CT, nvalid=1, vcol=0)
        syc = syc.reshape(BS, CT, C_CH)[:, 0]
        smix = jnp.concatenate([shm, sha.astype(BF16), syc], axis=1)
        smix = jnp.pad(smix, ((0, MS - BS), (0, 0)))

        xp, xs = _trunk_tail(xp, xs, mix, smix, mk.reshape(1, B, N_MEM, D), mv.reshape(1, B, N_MEM, D),
                             smem_k, smem_v, w, l, ln_g[l], ln_b[l], B=B, S=S, BS=BS)
        p_C.append(C1); p_n.append(n1.reshape(B, M_HEADS, M_DK)); p_m.append(m1.reshape(B, M_HEADS))
        p_conv.append(conv_tail)
        p_mk.append(mk.reshape(B, N_MEM, XA_HEADS, XA_DH)); p_mv.append(mv.reshape(B, N_MEM, XA_HEADS, XA_DH))
        s_C.append(Cs); s_n.append(ns.reshape(BS, M_HEADS, M_DK)); s_m.append(ms.reshape(BS, M_HEADS))
        s_conv.append(sconv_tail)
    xp32, xs32 = xp[0], xs[0]

    st = lambda xs: jnp.stack(xs)
    s_win = [shift_append(cache_win[i], st(new_rows[i])) for i in range(6)]
    return (xp32.reshape(B, S, D), xs32[:BS].reshape(BS, 1, D),
            st(p_win[0]), st(p_win[1]), st(p_win[2]), st(p_win[3]), st(p_win[4]), st(p_win[5]),
            st(p_C), st(p_n), st(p_m), st(p_conv), st(p_mk), st(p_mv),
            s_win[0], s_win[1], s_win[2], s_win[3], s_win[4], s_win[5],
            st(s_C), st(s_n), st(s_m), st(s_conv))
```

```python
import functools
import math

import jax
import jax.numpy as jnp
from jax import lax
from jax.experimental import pallas as pl
from jax.experimental.pallas import tpu as pltpu

F32 = jnp.float32
BF16 = jnp.bfloat16

D_MODEL = 4096
DEPTH = 2
M_HEADS, M_DK, M_DV, M_CHUNK = 4, 256, 512, 64
A_HEADS, A_DH = 8, 128
A_BRANCHES = ((128, 1), (512, 4), (2048, 16))
A_J = 128
C_CH, C_WIDTH = 1024, 31
N_BUCKETS, REL_MAX_DIST = 32, 2048
XA_HEADS = 4
XA_DH = D_MODEL // XA_HEADS
N_MEM = 256
D_FF = 4 * D_MODEL
DEEPNORM_ALPHA = (2 * DEPTH) ** 0.25
LN_EPS = 1e-5

N_MAIN = 2 * M_HEADS * M_DK + 2 * M_HEADS * M_DV
N_GATE = 2 * M_HEADS
N_ATT = 9 * A_HEADS * A_DH
COL_BLK = 1024
CONV_COL = N_MAIN // COL_BLK
MIX_ATT_COL = M_HEADS * M_DV // COL_BLK
MIX_CONV_COL = MIX_ATT_COL + 1
GATE_PAD = 128
SUBLANES = 8
CONV_HALO = 32
LN_ROWS = 16
SAMPLE_ROWS = 16
COPY_ROWS = 64

VMEM_LIMIT = 60000 * 1024
NEG_INF = float("-inf")


def _cparams(sem):
    return pltpu.CompilerParams(dimension_semantics=sem, vmem_limit_bytes=VMEM_LIMIT)


def _layer_norm_rows(z, g, b):
    mu = jnp.mean(z, axis=-1, keepdims=True)
    zc = z - mu
    var = jnp.mean(zc * zc, axis=-1, keepdims=True)
    return zc * lax.rsqrt(var + LN_EPS) * g + b


def _mm_kernel(*refs, nk, epilogue, scale, head_major, has_rider):
    refs = list(refs)
    x_ref = refs.pop(0)
    r_ref = refs.pop(0) if has_rider else None
    w_ref = refs.pop(0)
    res_ref = refs.pop(0) if epilogue == "resid" else None
    rres_ref = refs.pop(0) if epilogue == "resid" and has_rider else None
    o_ref = refs.pop(0)
    ro_ref = refs.pop(0) if has_rider else None
    scratch = refs

    def finish(acc, out_ref, as_head_major):
        if epilogue == "relu2":
            r = jnp.maximum(acc, 0.0)
            acc = r * r
        elif epilogue == "resid":
            acc = acc + DEEPNORM_ALPHA * (res_ref if out_ref is o_ref else rres_ref)[...]
        elif scale != 1.0:
            acc = acc * scale
        if as_head_major:
            for c in range(out_ref.shape[0]):
                out_ref[c] = acc[:, c * 128:(c + 1) * 128].astype(out_ref.dtype)
        else:
            out_ref[...] = acc.astype(out_ref.dtype)

    def accumulate(lhs_ref, out_ref, acc_ref, as_head_major):
        def prod():
            if len(lhs_ref.shape) == 2:
                return jnp.dot(lhs_ref[...].astype(BF16), w_ref[...], preferred_element_type=F32)
            cb = lhs_ref.shape[2]
            out = None
            for c in range(lhs_ref.shape[0]):
                part = jnp.dot(lhs_ref[c].astype(BF16), w_ref[c * cb:(c + 1) * cb, :], preferred_element_type=F32)
                out = part if out is None else out + part
            return out

        if nk == 1:
            finish(prod(), out_ref, as_head_major)
            return
        k = pl.program_id(2)

        @pl.when(k == 0)
        def _():
            acc_ref[...] = prod()

        @pl.when(k > 0)
        def _():
            acc_ref[...] += prod()

        @pl.when(k == nk - 1)
        def _():
            finish(acc_ref[...], out_ref, as_head_major)

    accumulate(x_ref, o_ref, scratch[0] if nk > 1 else None, head_major)
    if has_rider:
        @pl.when(pl.program_id(1) == 0)
        def _():
            accumulate(r_ref, ro_ref, scratch[1] if nk > 1 else None, False)


def matmul(x, w, layer, *, tm, tn, tk, out_dtype=F32, epilogue="none", scale=1.0, out_layout="rows",
           lhs_slabs=False, rider=None, res=None, rider_res=None):
    if lhs_slabs:
        nslab, M, cb = x.shape
        K = nslab * cb
    else:
        M, K = x.shape
    N = w.shape[2]
    tm, tn, tk = min(tm, M), min(tn, N), min(tk, K)
    assert M % tm == 0 and N % tn == 0 and K % tk == 0, (x.shape, w.shape, tm, tn, tk)
    nk = K // tk
    head_major = out_layout == "heads"
    if head_major:
        out_specs = pl.BlockSpec((tn // 128, tm, 128), lambda j, i, k: (j, i, 0))
        out_shape = jax.ShapeDtypeStruct((N // 128, M, 128), out_dtype)
    elif out_layout == "slabs":
        out_specs = pl.BlockSpec((None, tm, tn), lambda j, i, k: (j, i, 0))
        out_shape = jax.ShapeDtypeStruct((N // tn, M, tn), out_dtype)
    else:
        out_specs = pl.BlockSpec((tm, tn), lambda j, i, k: (i, j))
        out_shape = jax.ShapeDtypeStruct((M, N), out_dtype)
    if lhs_slabs:
        assert tk % cb == 0
        x_spec = pl.BlockSpec((tk // cb, tm, cb), lambda j, i, k: (k, i, 0))
    else:
        x_spec = pl.BlockSpec((tm, tk), lambda j, i, k: (i, k))
    w_spec = pl.BlockSpec((None, tk, tn), lambda j, i, k: (layer, k, j))
    scratch = [pltpu.VMEM((tm, tn), F32)] if nk > 1 else []
    in_specs, args = [x_spec], [x]
    if rider is not None:
        R = rider.shape[0]
        in_specs.append(pl.BlockSpec((R, tk), lambda j, i, k: (0, k)))
        args.append(rider)
        out_specs = (out_specs, pl.BlockSpec((R, tn), lambda j, i, k: (0, j)))
        out_shape = (out_shape, jax.ShapeDtypeStruct((R, N), out_dtype))
        scratch += [pltpu.VMEM((R, tn), F32)] if nk > 1 else []
    in_specs.append(w_spec)
    args.append(w)
    if epilogue == "resid":
        in_specs.append(pl.BlockSpec((tm, tn), lambda j, i, k: (i, j), pipeline_mode=pl.Buffered(1)))
        args.append(res)
        if rider is not None:
            in_specs.append(pl.BlockSpec((R, tn), lambda j, i, k: (0, j)))
            args.append(rider_res)
    return pl.pallas_call(
        functools.partial(_mm_kernel, nk=nk, epilogue=epilogue, scale=scale, head_major=head_major,
                          has_rider=rider is not None),
        grid=(N // tn, M // tm, nk),
        in_specs=in_specs, out_specs=out_specs, out_shape=out_shape, scratch_shapes=scratch,
        compiler_params=_cparams(("parallel", "arbitrary", "arbitrary")),
        name=f"mm_{epilogue}",
    )(*args)


def _post_ln_kernel(z_ref, g_ref, b_ref, o32_ref, o16_ref):
    def rows(i, carry):
        r = pl.ds(pl.multiple_of(i * LN_ROWS, LN_ROWS), LN_ROWS)
        z = jnp.concatenate([z_ref[c, r, :] for c in range(z_ref.shape[0])], axis=1)
        y = _layer_norm_rows(z, g_ref[...], b_ref[...])
        o32_ref[r, :] = y
        o16_ref[r, :] = y.astype(BF16)
        return carry
    lax.fori_loop(0, o32_ref.shape[0] // LN_ROWS, rows, 0, unroll=2)


def post_ln(z, g, b, *, tr):
    nslab, M, cb = z.shape
    N = nslab * cb
    tr = min(tr, M)
    assert M % tr == 0 and tr % (2 * LN_ROWS) == 0 or tr == LN_ROWS
    row = pl.BlockSpec((tr, N), lambda i: (i, 0))
    vec = pl.BlockSpec((1, N), lambda i: (0, 0))
    slabs = pl.BlockSpec((nslab, tr, cb), lambda i: (0, i, 0))
    return pl.pallas_call(
        _post_ln_kernel, grid=(M // tr,), in_specs=[slabs, vec, vec], out_specs=(row, row),
        out_shape=(jax.ShapeDtypeStruct((M, N), F32), jax.ShapeDtypeStruct((M, N), BF16)),
        compiler_params=_cparams(("parallel",)),
        name="post_ln",
    )(z, g.reshape(1, N).astype(F32), b.reshape(1, N).astype(F32))


def _mlstm_kernel(gb_ref, g_ref, q_ref, k_ref, v_ref, o_ref, ng_ref, c0_ref, n0_ref, m0_ref,
                  h_ref, c_ref, n_ref, m_ref, *, L, nvalid):
    c = pl.program_id(1)

    @pl.when(c == 0)
    def _():
        c_ref[...] = c0_ref[...]
        n_ref[...] = n0_ref[...]
        m_ref[...] = m0_ref[...]

    for hd in range(M_HEADS):
        _mlstm_head(hd, c, gb_ref, g_ref, q_ref, k_ref, v_ref, o_ref, ng_ref, h_ref, c_ref, n_ref, m_ref,
                    L=L, nvalid=nvalid)
    if h_ref.shape[1] > M_HEADS * M_DV:
        h_ref[:, M_HEADS * M_DV:] = jnp.zeros((L, h_ref.shape[1] - M_HEADS * M_DV), h_ref.dtype)


def _mlstm_head(hd, c, gb_ref, g_ref, q_ref, k_ref, v_ref, o_ref, ng_ref, h_ref, c_ref, n_ref, m_ref, *, L, nvalid):
    dk, dv = M_DK, M_DV
    ks, vs = slice(hd * dk, (hd + 1) * dk), slice(hd * dv, (hd + 1) * dv)
    ig = g_ref[0, 0, hd, pl.ds(c, 1), :] + gb_ref[0, hd]
    fg = g_ref[0, 1, hd, pl.ds(c, 1), :] + gb_ref[1, hd]
    lf = -(jnp.maximum(-fg, 0.0) + jnp.log1p(jnp.exp(-jnp.abs(fg))))
    if nvalid < L:
        lane = lax.broadcasted_iota(jnp.int32, (1, L), 1)
        ig = jnp.where(lane < nvalid, ig, NEG_INF)
        lf = jnp.where(lane < nvalid, lf, 0.0)
    qi = lax.broadcasted_iota(jnp.int32, (L, L), 0)
    si = lax.broadcasted_iota(jnp.int32, (L, L), 1)
    eye = qi == si
    tri = si <= qi
    lf_col = jnp.sum(jnp.where(eye, lf, 0.0), axis=1, keepdims=True)
    b_col = jnp.sum(jnp.where(tri, lf, 0.0), axis=1, keepdims=True)
    b_row = jnp.sum(jnp.where(qi <= si, lf_col, 0.0), axis=0, keepdims=True)
    m_prev = m_ref[0, hd]
    dmat = jnp.where(tri, b_col - b_row + ig, NEG_INF)
    inter = b_col + m_prev
    mt = jnp.maximum(inter, jnp.max(dmat, axis=1, keepdims=True))
    w_intra = jnp.exp(dmat - mt)
    w_inter = jnp.exp(inter - mt)

    q = q_ref[:, ks]
    kf = k_ref[:, ks] * (M_DK ** -0.5)
    per_slab = COL_BLK // dv
    vslab = lambda ref: ref[hd // per_slab, :, (hd % per_slab) * dv:(hd % per_slab + 1) * dv]
    qb, kb, vb = q.astype(BF16), kf.astype(BF16), vslab(v_ref).astype(BF16)
    c_prev = c_ref[0, hd]
    n_prev = n_ref[0, hd]
    sc = lax.dot_general(qb, kb, (((1,), (1,)), ((), ())), preferred_element_type=F32) * w_intra
    num = (w_inter * jnp.dot(qb, c_prev.astype(BF16), preferred_element_type=F32)
           + jnp.dot(sc.astype(BF16), vb, preferred_element_type=F32))
    den = w_inter * jnp.sum(q * n_prev, axis=1, keepdims=True) + jnp.sum(sc, axis=1, keepdims=True)
    h = num / jnp.maximum(jnp.abs(den), jnp.exp(-mt))

    m_new = mt[L - 1:L, :]
    b_last = b_col[L - 1:L, :]
    w_end = jnp.exp(b_last - b_row + ig - m_new)
    w_end_col = jnp.sum(jnp.where(eye, w_end, 0.0), axis=1, keepdims=True)
    decay = jnp.exp(b_last + m_prev - m_new)
    kw = kf * w_end_col
    c_ref[0, hd] = decay * c_prev + lax.dot_general(
        kw.astype(BF16), vb, (((0,), (0,)), ((), ())), preferred_element_type=F32)
    n_ref[0, hd] = decay * n_prev + jnp.sum(kw, axis=0, keepdims=True)
    m_ref[0, hd] = m_new

    hn = h * lax.rsqrt(jnp.mean(h * h, axis=-1, keepdims=True) + LN_EPS) * ng_ref[:, vs]
    h_ref[:, vs] = (jax.nn.sigmoid(vslab(o_ref)) * hn).astype(h_ref.dtype)


def mlstm(proj, gates, gate_b, norm_g, state, layer, *, B, S, L, nvalid, out_cols):
    nc = S // L
    H, dk, dv = M_HEADS, M_DK, M_DV
    if state is None:
        c0 = jnp.zeros((1, B, H, dk, dv), F32)
        n0 = jnp.zeros((1, B, H, 1, dk), F32)
        m0 = jnp.zeros((1, B, H, 1, 1), F32)
        layer = 0
    else:
        c0 = state[0]
        n0 = state[1].reshape(state[1].shape[0], B, H, 1, dk)
        m0 = state[2].reshape(state[2].shape[0], B, H, 1, 1)
    st_in = lambda b, c: (layer, b, 0, 0, 0)
    st4 = lambda b, c: (b, 0, 0, 0)
    row = lambda b, c: b * nc + c
    assert H * dk == COL_BLK and H * dv == 2 * COL_BLK
    return pl.pallas_call(
        functools.partial(_mlstm_kernel, L=L, nvalid=nvalid),
        grid=(B, nc),
        in_specs=[
            pl.BlockSpec(memory_space=pltpu.SMEM),
            pl.BlockSpec((1, 2, H, nc, L), lambda b, c: (b, 0, 0, 0, 0)),
            pl.BlockSpec((None, L, COL_BLK), lambda b, c: (0, row(b, c), 0)),
            pl.BlockSpec((None, L, COL_BLK), lambda b, c: (1, row(b, c), 0)),
            pl.BlockSpec((2, L, COL_BLK), lambda b, c: (1, row(b, c), 0)),
            pl.BlockSpec((2, L, COL_BLK), lambda b, c: (2, row(b, c), 0)),
            pl.BlockSpec((1, H * dv), lambda b, c: (0, 0)),
            pl.BlockSpec((None, 1, H, dk, dv), st_in),
            pl.BlockSpec((None, 1, H, 1, dk), st_in),
            pl.BlockSpec((None, 1, H, 1, 1), st_in),
        ],
        out_specs=(
            pl.BlockSpec((L, out_cols), lambda b, c: (row(b, c), 0)),
            pl.BlockSpec((1, H, dk, dv), st4),
            pl.BlockSpec((1, H, 1, dk), st4),
            pl.BlockSpec((1, H, 1, 1), st4),
        ),
        out_shape=(
            jax.ShapeDtypeStruct((B * S, out_cols), BF16),
            jax.ShapeDtypeStruct((B, H, dk, dv), F32),
            jax.ShapeDtypeStruct((B, H, 1, dk), F32),
            jax.ShapeDtypeStruct((B, H, 1, 1), F32),
        ),
        compiler_params=_cparams(("parallel", "arbitrary")),
        name="mlstm",
    )(gate_b, gates, proj, proj, proj, proj, norm_g.reshape(1, H * dv), c0, n0, m0)


def _attn_prompt_kernel(*refs, dil, heads, head0, first, last, with_prev):
    bias_ref, q_ref, kc_ref, vc_ref = refs[:4]
    refs = refs[4:]
    if with_prev:
        kp_ref, vp_ref = refs[:2]
        refs = refs[2:]
    if not first:
        num_in_ref, sm_in_ref = refs[:2]
        refs = refs[2:]
    if last:
        _, out_ref, o32_ref = refs
    else:
        num_out_ref, sm_out_ref = refs
    T = A_J
    lane = lax.broadcasted_iota(jnp.int32, (T, 128), 1)
    if with_prev:
        key_ok = (pl.program_id(2) > 0) | (lax.broadcasted_iota(jnp.int32, (T, 2 * T), 1) >= T)
    nt = (((1,), (1,)), ((), ()))
    for r in range(dil):
        rows = pl.ds(r, T, stride=dil) if dil > 1 else pl.ds(0, T)
        sm_old = None if first else sm_in_ref[rows, :]
        sm_new = jnp.zeros((T, 128), F32)
        for hl in range(heads):
            hg = head0 + hl
            qh = (q_ref[hl, rows, :] * (A_DH ** -0.5)).astype(BF16)
            if with_prev:
                kk = jnp.concatenate([kp_ref[hl, rows, :], kc_ref[hl, rows, :]], axis=0).astype(BF16)
                vv = jnp.concatenate([vp_ref[hl, rows, :], vc_ref[hl, rows, :]], axis=0).astype(BF16)
                logits = lax.dot_general(qh, kk, nt, preferred_element_type=F32) + bias_ref[hg]
                logits = jnp.where(key_ok, logits, NEG_INF)
            else:
                kk = kc_ref[hl, rows, :].astype(BF16)
                vv = vc_ref[hl, rows, :].astype(BF16)
                logits = lax.dot_general(qh, kk, nt, preferred_element_type=F32) + bias_ref[hg]
            mx = jnp.max(logits, axis=1, keepdims=True)
            p = jnp.exp(logits - mx)
            s = jnp.sum(p, axis=1, keepdims=True)
            o = jnp.dot(p.astype(BF16), vv, preferred_element_type=F32)
            if not first:
                s_old = sm_old[:, hg:hg + 1]
                m_old = sm_old[:, A_HEADS + hg:A_HEADS + hg + 1]
                m_all = jnp.maximum(m_old, mx)
                a_old = jnp.exp(m_old - m_all)
                a_new = jnp.exp(mx - m_all)
                o = a_old * num_in_ref[hl, rows, :] + a_new * o
                s = a_old * s_old + a_new * s
                mx = m_all
            if last:
                o32_ref[hl, rows, :] = o / s
            else:
                num_out_ref[hl, rows, :] = o
                sm_new = jnp.where(lane == hg, s, sm_new)
                sm_new = jnp.where(lane == A_HEADS + hg, mx, sm_new)
        if not last:
            sm_out_ref[rows, :] = sm_new
    if last:
        for hl in range(heads):
            out_ref[:, hl * A_DH:(hl + 1) * A_DH] = o32_ref[hl].astype(out_ref.dtype)


def attn_prompt_branch(att, bias_c, bias_p, state, mix, *, B, S, g, dil, hb):
    first, last = state is None, mix is not None
    span = A_J * dil
    nb = S // span
    with_prev = nb > 1
    nhb = A_HEADS // hb
    assert nhb == 1 or last
    cur = lambda part: pl.BlockSpec((hb, span, A_DH), lambda b, c, n: ((3 * g + part) * nhb + c, b * nb + n, 0))
    prev = lambda part: pl.BlockSpec(
        (hb, span, A_DH), lambda b, c, n: ((3 * g + part) * nhb + c, jnp.maximum(b * nb + n - 1, 0), 0))
    bias = jnp.concatenate([bias_p, bias_c], axis=2) if with_prev else bias_c
    num_spec = pl.BlockSpec((hb, span, A_DH), lambda b, c, n: (c, b * nb + n, 0))
    sm_spec = pl.BlockSpec((span, 128), lambda b, c, n: (b * nb + n, 0))
    in_specs = [pl.BlockSpec(bias.shape, lambda b, c, n: (0, 0, 0)), cur(0), cur(1), cur(2)]
    args = [bias, att, att, att]
    if with_prev:
        in_specs += [prev(1), prev(2)]
        args += [att, att]
    if not first:
        in_specs += [num_spec, sm_spec]
        args += list(state)
    aliases = {}
    if last:
        in_specs.append(pl.BlockSpec(memory_space=pl.ANY))
        args.append(mix)
        aliases = {len(args) - 1: 0}
        out_specs = pl.BlockSpec((span, hb * A_DH), lambda b, c, n: (b * nb + n, MIX_ATT_COL * nhb + c))
        out_shape = jax.ShapeDtypeStruct(mix.shape, mix.dtype)
    else:
        out_specs = (num_spec, sm_spec)
        out_shape = (jax.ShapeDtypeStruct((A_HEADS, B * S, A_DH), F32), jax.ShapeDtypeStruct((B * S, 128), F32))

    def body(*refs):
        kw = dict(dil=dil, heads=hb, first=first, last=last, with_prev=with_prev)
        if nhb == 1:
            _attn_prompt_kernel(*refs, head0=0, **kw)
        else:
            for cb in range(nhb):
                @pl.when(pl.program_id(1) == cb)
                def _(cb=cb):
                    _attn_prompt_kernel(*refs, head0=cb * hb, **kw)

    return pl.pallas_call(
        body, grid=(B, nhb, nb), in_specs=in_specs, out_specs=out_specs, out_shape=out_shape,
        input_output_aliases=aliases,
        scratch_shapes=[pltpu.VMEM((hb, span, A_DH), F32)] if last else [],
        compiler_params=_cparams(("parallel", "parallel", "arbitrary")),
        name=f"attn_prompt_{g}",
    )(*args)


def _attn_decode_kernel(bias_ref, a_ref, k1_ref, v1_ref, k2_ref, v2_ref, k3_ref, v3_ref, out_ref):
    caches = ((k1_ref, v1_ref), (k2_ref, v2_ref), (k3_ref, v3_ref))
    nt = (((1,), (1,)), ((), ()))
    hw = A_HEADS * A_DH
    for h in range(A_HEADS):
        outs, dens, maxs = [], [], []
        for g in range(3):
            base = 3 * g * hw + h * A_DH
            qf = (a_ref[0, :, base:base + A_DH] * (A_DH ** -0.5)).astype(BF16)
            kn = a_ref[0, :, base + hw:base + hw + A_DH].astype(BF16).astype(F32)
            vn = a_ref[0, :, base + 2 * hw:base + 2 * hw + A_DH].astype(BF16).astype(F32)
            kc = caches[g][0][:, h, :].astype(BF16)
            vc = caches[g][1][:, h, :].astype(BF16)
            q16 = jnp.broadcast_to(qf, (16, A_DH))
            lc = lax.dot_general(q16, kc, nt, preferred_element_type=F32)[0:1, :] + bias_ref[g, h:h + 1, 0:A_J]
            ln = (jnp.sum(qf.astype(F32) * kn, axis=1, keepdims=True)
                  + bias_ref[g, h:h + 1, A_J:A_J + 1])
            mx = jnp.maximum(jnp.max(lc, axis=1, keepdims=True), ln)
            pc = jnp.exp(lc - mx)
            pn = jnp.exp(ln - mx)
            p16 = jnp.broadcast_to(pc.astype(BF16), (16, A_J))
            o = jnp.dot(p16, vc, preferred_element_type=F32)[0:1, :] + pn.astype(BF16).astype(F32) * vn
            outs.append(o)
            dens.append(jnp.sum(pc, axis=1, keepdims=True) + pn)
            maxs.append(mx)
        m_all = jnp.maximum(jnp.maximum(maxs[0], maxs[1]), maxs[2])
        wts = [jnp.exp(m - m_all) for m in maxs]
        num = wts[0] * outs[0] + wts[1] * outs[1] + wts[2] * outs[2]
        den = wts[0] * dens[0] + wts[1] * dens[1] + wts[2] * dens[2]
        out_ref[0, :, h * A_DH:(h + 1) * A_DH] = num / den


def attn_decode(att, caches, layer, bias_dec):
    B = att.shape[0]
    hw = A_HEADS * A_DH
    in_specs = [pl.BlockSpec(bias_dec.shape, lambda b: (0, 0, 0)),
                pl.BlockSpec((1, 1, N_ATT), lambda b: (b, 0, 0))]
    args = [bias_dec, att.reshape(B, 1, N_ATT)]
    for g, (window, dil) in enumerate(A_BRANCHES):
        for t in caches[2 * g:2 * g + 2]:
            assert t.shape[2] == window
            in_specs.append(pl.BlockSpec((None, None, A_J, None, A_HEADS, A_DH), lambda b: (layer, b, 0, 0, 0, 0)))
            args.append(t.reshape(t.shape[0], B, window // dil, dil, A_HEADS, A_DH))
    out = pl.pallas_call(
        _attn_decode_kernel, grid=(B,), in_specs=in_specs,
        out_specs=pl.BlockSpec((1, 1, hw), lambda b: (b, 0, 0)),
        out_shape=jax.ShapeDtypeStruct((B, 1, hw), F32),
        compiler_params=_cparams(("parallel",)),
        name="attn_decode",
    )(*args)
    return out.reshape(B, hw)


def _shift_kernel(c_ref, new_ref, o_ref):
    P = c_ref.shape[0]
    R = min(COPY_ROWS, P)

    def chunk(i, carry):
        o_ref[pl.ds(i * R, R)] = c_ref[pl.ds(i * R + 1, R)]
        return carry
    lax.fori_loop(0, P // R - 1, chunk, 0)
    o_ref[P - R:P - 1] = c_ref[P - R + 1:P]
    o_ref[P - 1:P] = new_ref[...]


def shift_append(cache, new_rows):
    Dp, B, P, H, Dh = cache.shape
    blk = lambda rows: pl.BlockSpec((None, None, rows, H, Dh), lambda l, b: (l, b, 0, 0, 0))
    return pl.pallas_call(
        _shift_kernel, grid=(Dp, B), in_specs=[blk(P), blk(1)], out_specs=blk(P),
        out_shape=jax.ShapeDtypeStruct(cache.shape, cache.dtype),
        compiler_params=_cparams(("parallel", "parallel")),
        name="shift_append",
    )(cache, new_rows.astype(cache.dtype))


def _conv_kernel(cv_ref, cg_ref, hv_ref, hg_ref, past_ref, cw_ref, cb_ref, lg_ref, lb_ref, *rest,
                 TS, RC, nvalid):
    y_ref, tail_ref, xs_ref, sh_ref = rest[-4:]
    t = pl.program_id(1)
    H = CONV_HALO

    @pl.when(t == 0)
    def _():
        xs_ref[0:H, :] = past_ref[0]

    @pl.when(t > 0)
    def _():
        xs_ref[0:H, :] = hv_ref[...] * jax.nn.sigmoid(hg_ref[...])

    xs_ref[H:H + TS, :] = cv_ref[...] * jax.nn.sigmoid(cg_ref[...])
    lead = H - (C_WIDTH - 1)
    SUB = SUBLANES
    span = sh_ref.shape[1]
    for p in range(1, SUB):
        sh_ref[p - 1] = xs_ref[p:p + span, :]
    for rc in range(TS // RC):
        acc = jnp.zeros((RC, C_CH), F32)
        for w in range(C_WIDTH):
            m, p = divmod(lead + w, SUB)
            r0 = rc * RC + SUB * m
            rows = xs_ref[r0:r0 + RC, :] if p == 0 else sh_ref[p - 1, r0:r0 + RC, :]
            acc = acc + rows * cw_ref[w:w + 1, :]
        y = _layer_norm_rows(acc + cb_ref[...], lg_ref[...], lb_ref[...])
        y_ref[rc * RC:(rc + 1) * RC, :] = (y * jax.nn.sigmoid(y)).astype(y_ref.dtype)
    tail_ref[0] = xs_ref[nvalid:nvalid + H, :]


def conv_block(proj, past, conv_w, conv_b, ln_g, ln_b, mix, *, B, S, TS, nvalid, vcol):
    nt = S // TS
    H = CONV_HALO
    RC = min(TS, 32)
    past_p = jnp.pad(past.astype(F32), ((0, 0), (H - (C_WIDTH - 1), 0), (0, 0)))
    cur = lambda col: pl.BlockSpec((None, TS, C_CH), lambda b, t: (col, b * nt + t, 0))
    halo_rows = min(H, B * S)
    halo = lambda col: pl.BlockSpec(
        (None, halo_rows, C_CH), lambda b, t: (col, jnp.maximum((b * S + t * TS) // H - 1, 0), 0))
    vec = pl.BlockSpec((1, C_CH), lambda b, t: (0, 0))
    in_specs = [cur(vcol), cur(vcol + 1), halo(vcol), halo(vcol + 1),
                pl.BlockSpec((1, H, C_CH), lambda b, t: (b, 0, 0)),
                pl.BlockSpec((H, C_CH), lambda b, t: (0, 0)), vec, vec, vec]
    args = [proj, proj, proj, proj, past_p, jnp.pad(conv_w.astype(F32), ((0, H - C_WIDTH), (0, 0))),
            conv_b.reshape(1, C_CH), ln_g.reshape(1, C_CH), ln_b.reshape(1, C_CH)]
    aliases = {}
    if mix is None:
        y_spec = pl.BlockSpec((TS, C_CH), lambda b, t: (b * nt + t, 0))
        y_shape = jax.ShapeDtypeStruct((B * S, C_CH), BF16)
    else:
        in_specs.append(pl.BlockSpec(memory_space=pl.ANY))
        args.append(mix)
        aliases = {len(args) - 1: 0}
        y_spec = pl.BlockSpec((TS, C_CH), lambda b, t: (b * nt + t, MIX_CONV_COL))
        y_shape = jax.ShapeDtypeStruct(mix.shape, mix.dtype)
    y, tail = pl.pallas_call(
        functools.partial(_conv_kernel, TS=TS, RC=RC, nvalid=nvalid),
        grid=(B, nt), in_specs=in_specs,
        out_specs=(y_spec, pl.BlockSpec((1, H, C_CH), lambda b, t: (b, 0, 0))),
        out_shape=(y_shape, jax.ShapeDtypeStruct((B, H, C_CH), F32)),
        scratch_shapes=[pltpu.VMEM((H + TS, C_CH), F32),
                        pltpu.VMEM((SUBLANES - 1, H + TS - SUBLANES, C_CH), F32)],
        input_output_aliases=aliases,
        compiler_params=_cparams(("parallel", "arbitrary")),
        name="conv_ln_silu",
    )(*args)
    return y, tail[:, H - (C_WIDTH - 1):]


def _xattn_kernel(q_ref, mk_ref, mv_ref, o_ref):
    q = q_ref[...]
    logits = lax.dot_general(q, mk_ref[...].astype(BF16), (((1,), (1,)), ((), ())),
                             preferred_element_type=F32)
    e = jnp.exp(logits - jnp.max(logits, axis=-1, keepdims=True))
    p = e / jnp.sum(e, axis=-1, keepdims=True)
    o_ref[0] = jnp.dot(p.astype(BF16), mv_ref[...].astype(BF16), preferred_element_type=F32).astype(o_ref.dtype)


def cross_attn(q, mk, mv, layer, *, B, S, TS):
    TS = min(TS, S)
    nt = S // TS
    mem = pl.BlockSpec((None, None, N_MEM, XA_DH), lambda b, h, t: (layer, b, 0, h))
    return pl.pallas_call(
        _xattn_kernel, grid=(B, XA_HEADS, nt),
        in_specs=[pl.BlockSpec((None, TS, XA_DH), lambda b, h, t: (h, b * nt + t, 0)), mem, mem],
        out_specs=pl.BlockSpec((1, TS, XA_DH), lambda b, h, t: (b, t, h)),
        out_shape=jax.ShapeDtypeStruct((B, S, XA_HEADS * XA_DH), BF16),
        compiler_params=_cparams(("parallel", "parallel", "arbitrary")),
        name="cross_attn",
    )(q, mk, mv)


def _rel_bucket(dist):
    exact = N_BUCKETS // 2
    lg = jnp.log(jnp.maximum(dist, 1).astype(F32) / exact) / math.log(REL_MAX_DIST / exact)
    large = jnp.minimum(exact + (lg * (N_BUCKETS - exact)).astype(jnp.int32), N_BUCKETS - 1)
    return jnp.where(dist < exact, dist, large)


def _bias_tables(rel_bias):
    J = A_J
    cur, prev, dec = [], [], []
    for g, (window, dil) in enumerate(A_BRANCHES):
        table = rel_bias[:, g * A_HEADS:(g + 1) * A_HEADS].astype(F32)
        bucket = _rel_bucket(dil * jnp.arange(window // dil + 1))
        onehot = bucket[:, None] == jnp.arange(N_BUCKETS)[None, :]
        bias_j = jnp.sum(jnp.where(onehot[:, :, None], table[None], 0.0), axis=1)
        rev = bias_j.T[:, ::-1]
        u = jnp.concatenate([rev, jnp.full((A_HEADS, J), NEG_INF, F32)], axis=1)
        tile = jnp.broadcast_to(u[:, None, :], (A_HEADS, J, 2 * J + 1)).reshape(A_HEADS, J * (2 * J + 1))
        tile = tile[:, :J * 2 * J].reshape(A_HEADS, J, 2 * J)
        prev.append(tile[:, :, :J])
        cur.append(tile[:, :, J:])
        dec.append(jnp.pad(rev, ((0, 0), (0, 2 * J - (J + 1)))))
    return cur, prev, jnp.stack(dec)


def _prep_weights(w_in, w_out, xa_wq, xa_wk, xa_wv, xa_wo, ffn_w1, ffn_w2):
    a0 = N_MAIN + N_GATE
    w_gate = jnp.pad(w_in[:, :, N_MAIN:a0], ((0, 0), (0, 0), (0, GATE_PAD - N_GATE))).astype(BF16)
    w_main = jnp.concatenate([w_in[:, :, :N_MAIN], w_in[:, :, a0 + N_ATT:]], axis=2).astype(BF16)
    w_att = w_in[:, :, a0:a0 + N_ATT].astype(BF16)
    cast = lambda w: w.astype(BF16)
    return dict(main=w_main, att=w_att, gate=w_gate,
                out=cast(w_out), wq=cast(xa_wq), wk=cast(xa_wk), wv=cast(xa_wv),
                wo=cast(xa_wo), w1=cast(ffn_w1), w2=cast(ffn_w2))


def _gate_layout(gmat, B, S, L):
    g = gmat[:, :N_GATE].reshape(B, S // L, L, 2, M_HEADS)
    return g.transpose(0, 3, 4, 1, 2)


def _trunk_tail(xp, xs, mix, smix, mk, mv, smk, smv, w, l, lg, lb, *, B, S, BS):
    mm = functools.partial(matmul, tm=1024, tn=1024, tk=D_MODEL)
    MS = xs[0].shape[0]

    def sublayer_ln(lhs, slhs, wname, i, **kw):
        z, sz = mm(lhs, w[wname], l, epilogue="resid", out_layout="slabs", rider=slhs,
                   res=xp[0], rider_res=xs[0], **kw)
        return post_ln(z, lg[i], lb[i], tr=256), post_ln(_to_slabs(sz), lg[i], lb[i], tr=MS)

    xp, xs = sublayer_ln(mix, smix, "out", 0)
    q, sq = mm(xp[1], w["wq"], l, out_dtype=BF16, scale=XA_DH ** -0.5, out_layout="slabs", rider=xs[1])
    o = cross_attn(q, mk, mv, 0, B=B, S=S, TS=512).reshape(B * S, D_MODEL)
    sq = jnp.zeros((BS, MS, D_MODEL), BF16).at[:, 0].set(sq[:BS]).reshape(BS * MS, D_MODEL)
    so = jnp.pad(cross_attn(_to_slabs(sq), smk, smv, l, B=BS, S=MS, TS=MS)[:, 0], ((0, MS - BS), (0, 0)))
    xp, xs = sublayer_ln(o, so, "wo", 1)
    hid, shid = mm(xp[1], w["w1"], l, out_dtype=BF16, epilogue="relu2", out_layout="slabs", rider=xs[1])
    xp, xs = sublayer_ln(hid, shid, "w2", 2, lhs_slabs=True)
    return xp, xs


def _to_slabs(rows):
    M, N = rows.shape
    return rows.reshape(M, N // COL_BLK, COL_BLK).transpose(1, 0, 2)


def kernel(x_prompt, x_sample, mem_prompt, cache_win_k1, cache_win_v1, cache_win_k2, cache_win_v2, cache_win_k3, cache_win_v3, state_mlstm_C, state_mlstm_n, state_mlstm_m, state_conv, cache_mem_k, cache_mem_v, rel_bias, w_in, mlstm_gate_bias, mlstm_norm_g, conv_w, conv_b, conv_ln_g, conv_ln_b, w_out, xa_wq, xa_wk, xa_wv, xa_wo, ffn_w1, ffn_w2, ln_g, ln_b):
    B, S, D = x_prompt.shape
    BS = x_sample.shape[0]
    MS = SAMPLE_ROWS
    L = math.gcd(S, M_CHUNK)
    bias_c, bias_p, bias_dec = _bias_tables(rel_bias)
    cache_win = (cache_win_k1, cache_win_v1, cache_win_k2, cache_win_v2, cache_win_k3, cache_win_v3)
    sample_state = (state_mlstm_C.astype(F32), state_mlstm_n.astype(F32), state_mlstm_m.astype(F32))
    smem_k = cache_mem_k.reshape(DEPTH, BS, N_MEM, D)
    smem_v = cache_mem_v.reshape(DEPTH, BS, N_MEM, D)

    xp32 = x_prompt.reshape(B * S, D)
    xp16 = xp32.astype(BF16)
    xs32 = jnp.pad(x_sample.reshape(BS, D), ((0, MS - BS), (0, 0)))
    xs16 = xs32.astype(BF16)
    mem16 = mem_prompt.reshape(B * N_MEM, D).astype(BF16)

    p_win = [[] for _ in range(6)]
    new_rows = [[] for _ in range(6)]
    p_C, p_n, p_m, p_conv, p_mk, p_mv = [], [], [], [], [], []
    s_C, s_n, s_m, s_conv = [], [], [], []
    w = _prep_weights(w_in, w_out, xa_wq, xa_wk, xa_wv, xa_wo, ffn_w1, ffn_w2)
    mm = functools.partial(matmul, tm=1024, tn=1024, tk=D)
    xp, xs = (xp32, xp16), (xs32, xs16)
    for l in range(DEPTH):
        gate_b = mlstm_gate_bias[l].astype(F32)
        main, smain = mm(xp[1], w["main"], l, out_layout="slabs", rider=xs[1])
        att, satt = mm(xp[1], w["att"], l, out_layout="heads", rider=xs[1])
        gmat, sgm = mm(xp[1], w["gate"], l, rider=xs[1])
        satt = satt[:BS]

        mix, C1, n1, m1 = mlstm(main, _gate_layout(gmat, B, S, L), gate_b, mlstm_norm_g[l], None, 0,
                                B=B, S=S, L=L, nvalid=L, out_cols=D)
        state = None
        for g, (window, dil) in enumerate(A_BRANCHES):
            last = g == len(A_BRANCHES) - 1
            out = attn_prompt_branch(att, bias_c[g], bias_p[g], state, mix if last else None,
                                     B=B, S=S, g=g, dil=dil, hb=A_HEADS // 2 if last else A_HEADS)
            if last:
                mix = out
            else:
                state = out
            keep = min(window, S)
            for j in range(2):
                h0 = (3 * g + 1 + j) * A_HEADS
                kv = att[h0:h0 + A_HEADS].reshape(A_HEADS, B, S, A_DH)[:, :, S - keep:]
                p_win[2 * g + j].append(kv.transpose(1, 2, 0, 3))
        mix, conv_tail = conv_block(main, jnp.zeros((B, C_WIDTH - 1, C_CH), F32), conv_w[l], conv_b[l],
                                    conv_ln_g[l], conv_ln_b[l], mix, B=B, S=S, TS=128, nvalid=128, vcol=CONV_COL)
        mk = mm(mem16, w["wk"], l)
        mv = mm(mem16, w["wv"], l)

        LS = M_CHUNK
        main_pad = jnp.zeros((BS, LS, N_MAIN), F32).at[:, 0].set(smain[:BS, :N_MAIN]).reshape(BS * LS, N_MAIN)
        gate_pad = jnp.zeros((BS, LS, GATE_PAD), F32).at[:, 0].set(sgm[:BS]).reshape(BS * LS, GATE_PAD)
        shm, Cs, ns, ms = mlstm(_to_slabs(main_pad), _gate_layout(gate_pad, BS, LS, LS), gate_b, mlstm_norm_g[l],
                                sample_state, l, B=BS, S=LS, L=LS, nvalid=1, out_cols=M_HEADS * M_DV)
        shm = shm.reshape(BS, LS, M_HEADS * M_DV)[:, 0]
        sha = attn_decode(satt, cache_win, l, bias_dec)
        for g in range(len(A_BRANCHES)):
            for j in range(2):
                c0 = (3 * g + 1 + j) * COL_BLK
                new_rows[2 * g + j].append(satt[:, c0:c0 + COL_BLK].reshape(BS, 1, A_HEADS, A_DH))
        CT = 8
        cpad = jnp.zeros((BS, CT, 2 * C_CH), F32).at[:, 0].set(smain[:BS, N_MAIN:]).reshape(BS * CT, 2 * C_CH)
        syc, sconv_tail = conv_block(_to_slabs(cpad), state_conv[l], conv_w[l], conv_b[l], conv_ln_g[l], conv_ln_b[l], None,
                                     B=BS, S=CT, TS=CT, nvalid=1, vcol=0)
        syc = syc.reshape(BS, CT, C_CH)[:, 0]
        smix = jnp.concatenate([shm, sha.astype(BF16), syc], axis=1)
        smix = jnp.pad(smix, ((0, MS - BS), (0, 0)))

        xp, xs = _trunk_tail(xp, xs, mix, smix, mk.reshape(1, B, N_MEM, D), mv.reshape(1, B, N_MEM, D),
                             smem_k, smem_v, w, l, ln_g[l], ln_b[l], B=B, S=S, BS=BS)
        p_C.append(C1); p_n.append(n1.reshape(B, M_HEADS, M_DK)); p_m.append(m1.reshape(B, M_HEADS))
        p_conv.append(conv_tail)
        p_mk.append(mk.reshape(B, N_MEM, XA_HEADS, XA_DH)); p_mv.append(mv.reshape(B, N_MEM, XA_HEADS, XA_DH))
        s_C.append(Cs); s_n.append(ns.reshape(BS, M_HEADS, M_DK)); s_m.append(ms.reshape(BS, M_HEADS))
        s_conv.append(sconv_tail)
    xp32, xs32 = xp[0], xs[0]

    st = lambda xs: jnp.stack(xs)
    s_win = [shift_append(cache_win[i], st(new_rows[i])) for i in range(6)]
    return (xp32.reshape(B, S, D), xs32[:BS].reshape(BS, 1, D),
            st(p_win[0]), st(p_win[1]), st(p_win[2]), st(p_win[3]), st(p_win[4]), st(p_win[5]),
            st(p_C), st(p_n), st(p_m), st(p_conv), st(p_mk), st(p_mv),
            s_win[0], s_win[1], s_win[2], s_win[3], s_win[4], s_win[5],
            st(s_C), st(s_n), st(s_m), st(s_conv))
```

```python
import functools
import math

import jax
import jax.numpy as jnp
from jax import lax
from jax.experimental import pallas as pl
from jax.experimental.pallas import tpu as pltpu

F32 = jnp.float32
BF16 = jnp.bfloat16

D_MODEL = 4096
DEPTH = 2
M_HEADS, M_DK, M_DV, M_CHUNK = 4, 256, 512, 64
A_HEADS, A_DH = 8, 128
A_BRANCHES = ((128, 1), (512, 4), (2048, 16))
A_J = 128
C_CH, C_WIDTH = 1024, 31
N_BUCKETS, REL_MAX_DIST = 32, 2048
XA_HEADS = 4
XA_DH = D_MODEL // XA_HEADS
N_MEM = 256
D_FF = 4 * D_MODEL
DEEPNORM_ALPHA = (2 * DEPTH) ** 0.25
LN_EPS = 1e-5

N_MAIN = 2 * M_HEADS * M_DK + 2 * M_HEADS * M_DV
N_GATE = 2 * M_HEADS
N_ATT = 9 * A_HEADS * A_DH
COL_BLK = 1024
CONV_COL = N_MAIN // COL_BLK
MIX_ATT_COL = M_HEADS * M_DV // COL_BLK
MIX_CONV_COL = MIX_ATT_COL + 1
GATE_PAD = 128
SUBLANES = 8
CONV_HALO = 32
LN_ROWS = 16
SAMPLE_ROWS = 16
COPY_ROWS = 64

VMEM_LIMIT = 56 * 1024 * 1024
NEG_INF = float("-inf")


def _cparams(sem):
    return pltpu.CompilerParams(dimension_semantics=sem, vmem_limit_bytes=VMEM_LIMIT)


def _layer_norm_rows(z, g, b):
    mu = jnp.mean(z, axis=-1, keepdims=True)
    zc = z - mu
    var = jnp.mean(zc * zc, axis=-1, keepdims=True)
    return zc * lax.rsqrt(var + LN_EPS) * g + b


def _mm_kernel(*refs, nk, epilogue, scale, head_major, has_rider):
    if has_rider:
        x_ref, r_ref, w_ref, o_ref, ro_ref = refs[:5]
        scratch = refs[5:]
    else:
        x_ref, w_ref, o_ref = refs[:3]
        scratch = refs[3:]

    def finish(acc, out_ref, as_head_major):
        if epilogue == "relu2":
            r = jnp.maximum(acc, 0.0)
            acc = r * r
        elif scale != 1.0:
            acc = acc * scale
        if as_head_major:
            for c in range(out_ref.shape[0]):
                out_ref[c] = acc[:, c * 128:(c + 1) * 128].astype(out_ref.dtype)
        else:
            out_ref[...] = acc.astype(out_ref.dtype)

    def accumulate(lhs_ref, out_ref, acc_ref, as_head_major):
        prod = lambda: jnp.dot(lhs_ref[...].astype(BF16), w_ref[...], preferred_element_type=F32)
        if nk == 1:
            finish(prod(), out_ref, as_head_major)
            return
        k = pl.program_id(2)

        @pl.when(k == 0)
        def _():
            acc_ref[...] = prod()

        @pl.when(k > 0)
        def _():
            acc_ref[...] += prod()

        @pl.when(k == nk - 1)
        def _():
            finish(acc_ref[...], out_ref, as_head_major)

    accumulate(x_ref, o_ref, scratch[0] if nk > 1 else None, head_major)
    if has_rider:
        @pl.when(pl.program_id(1) == 0)
        def _():
            accumulate(r_ref, ro_ref, scratch[1] if nk > 1 else None, False)


def matmul(x, w, layer, *, tm, tn, tk, out_dtype=F32, epilogue="none", scale=1.0, head_major=False, rider=None):
    M, K = x.shape
    N = w.shape[2]
    tm, tn, tk = min(tm, M), min(tn, N), min(tk, K)
    assert M % tm == 0 and N % tn == 0 and K % tk == 0, (x.shape, w.shape, tm, tn, tk)
    nk = K // tk
    if head_major:
        out_specs = pl.BlockSpec((tn // 128, tm, 128), lambda j, i, k: (j, i, 0))
        out_shape = jax.ShapeDtypeStruct((N // 128, M, 128), out_dtype)
    else:
        out_specs = pl.BlockSpec((tm, tn), lambda j, i, k: (i, j))
        out_shape = jax.ShapeDtypeStruct((M, N), out_dtype)
    x_spec = pl.BlockSpec((tm, tk), lambda j, i, k: (i, k))
    w_spec = pl.BlockSpec((None, tk, tn), lambda j, i, k: (layer, k, j))
    scratch = [pltpu.VMEM((tm, tn), F32)] if nk > 1 else []
    if rider is None:
        in_specs, args = [x_spec, w_spec], (x, w)
    else:
        R = rider.shape[0]
        in_specs, args = [x_spec, pl.BlockSpec((R, tk), lambda j, i, k: (0, k)), w_spec], (x, rider, w)
        out_specs = (out_specs, pl.BlockSpec((R, tn), lambda j, i, k: (0, j)))
        out_shape = (out_shape, jax.ShapeDtypeStruct((R, N), out_dtype))
        scratch += [pltpu.VMEM((R, tn), F32)] if nk > 1 else []
    return pl.pallas_call(
        functools.partial(_mm_kernel, nk=nk, epilogue=epilogue, scale=scale, head_major=head_major,
                          has_rider=rider is not None),
        grid=(N // tn, M // tm, nk),
        in_specs=in_specs, out_specs=out_specs, out_shape=out_shape, scratch_shapes=scratch,
        compiler_params=_cparams(("parallel", "arbitrary", "arbitrary")),
        name=f"mm_{epilogue}",
    )(*args)


def _post_ln_kernel(res_ref, sub_ref, g_ref, b_ref, o32_ref, o16_ref):
    def rows(i, carry):
        r = pl.ds(pl.multiple_of(i * LN_ROWS, LN_ROWS), LN_ROWS)
        y = _layer_norm_rows(DEEPNORM_ALPHA * res_ref[r, :] + sub_ref[r, :], g_ref[...], b_ref[...])
        o32_ref[r, :] = y
        o16_ref[r, :] = y.astype(BF16)
        return carry
    lax.fori_loop(0, o32_ref.shape[0] // LN_ROWS, rows, 0, unroll=2)


def post_ln(res, sub, g, b, *, tr):
    M, N = res.shape
    tr = min(tr, M)
    assert M % tr == 0 and tr % (2 * LN_ROWS) == 0 or tr == LN_ROWS
    row = pl.BlockSpec((tr, N), lambda i: (i, 0))
    vec = pl.BlockSpec((1, N), lambda i: (0, 0))
    return pl.pallas_call(
        _post_ln_kernel, grid=(M // tr,), in_specs=[row, row, vec, vec], out_specs=(row, row),
        out_shape=(jax.ShapeDtypeStruct((M, N), F32), jax.ShapeDtypeStruct((M, N), BF16)),
        compiler_params=_cparams(("parallel",)),
        name="post_ln",
    )(res, sub, g.reshape(1, N).astype(F32), b.reshape(1, N).astype(F32))


def _mlstm_kernel(gb_ref, g_ref, q_ref, k_ref, v_ref, o_ref, ng_ref, c0_ref, n0_ref, m0_ref,
                  h_ref, c_ref, n_ref, m_ref, *, L, nvalid):
    c = pl.program_id(1)

    @pl.when(c == 0)
    def _():
        c_ref[...] = c0_ref[...]
        n_ref[...] = n0_ref[...]
        m_ref[...] = m0_ref[...]

    for hd in range(M_HEADS):
        _mlstm_head(hd, c, gb_ref, g_ref, q_ref, k_ref, v_ref, o_ref, ng_ref, h_ref, c_ref, n_ref, m_ref,
                    L=L, nvalid=nvalid)
    if h_ref.shape[1] > M_HEADS * M_DV:
        h_ref[:, M_HEADS * M_DV:] = jnp.zeros((L, h_ref.shape[1] - M_HEADS * M_DV), h_ref.dtype)


def _mlstm_head(hd, c, gb_ref, g_ref, q_ref, k_ref, v_ref, o_ref, ng_ref, h_ref, c_ref, n_ref, m_ref, *, L, nvalid):
    dk, dv = M_DK, M_DV
    ks, vs = slice(hd * dk, (hd + 1) * dk), slice(hd * dv, (hd + 1) * dv)
    ig = g_ref[0, 0, hd, pl.ds(c, 1), :] + gb_ref[0, hd]
    fg = g_ref[0, 1, hd, pl.ds(c, 1), :] + gb_ref[1, hd]
    lf = -(jnp.maximum(-fg, 0.0) + jnp.log1p(jnp.exp(-jnp.abs(fg))))
    if nvalid < L:
        lane = lax.broadcasted_iota(jnp.int32, (1, L), 1)
        ig = jnp.where(lane < nvalid, ig, NEG_INF)
        lf = jnp.where(lane < nvalid, lf, 0.0)
    qi = lax.broadcasted_iota(jnp.int32, (L, L), 0)
    si = lax.broadcasted_iota(jnp.int32, (L, L), 1)
    eye = qi == si
    tri = si <= qi
    lf_col = jnp.sum(jnp.where(eye, lf, 0.0), axis=1, keepdims=True)
    b_col = jnp.sum(jnp.where(tri, lf, 0.0), axis=1, keepdims=True)
    b_row = jnp.sum(jnp.where(qi <= si, lf_col, 0.0), axis=0, keepdims=True)
    m_prev = m_ref[0, hd]
    dmat = jnp.where(tri, b_col - b_row + ig, NEG_INF)
    inter = b_col + m_prev
    mt = jnp.maximum(inter, jnp.max(dmat, axis=1, keepdims=True))
    w_intra = jnp.exp(dmat - mt)
    w_inter = jnp.exp(inter - mt)

    q = q_ref[:, ks]
    kf = k_ref[:, ks] * (M_DK ** -0.5)
    qb, kb, vb = q.astype(BF16), kf.astype(BF16), v_ref[:, vs].astype(BF16)
    c_prev = c_ref[0, hd]
    n_prev = n_ref[0, hd]
    sc = lax.dot_general(qb, kb, (((1,), (1,)), ((), ())), preferred_element_type=F32) * w_intra
    num = (w_inter * jnp.dot(qb, c_prev.astype(BF16), preferred_element_type=F32)
           + jnp.dot(sc.astype(BF16), vb, preferred_element_type=F32))
    den = w_inter * jnp.sum(q * n_prev, axis=1, keepdims=True) + jnp.sum(sc, axis=1, keepdims=True)
    h = num / jnp.maximum(jnp.abs(den), jnp.exp(-mt))

    m_new = mt[L - 1:L, :]
    b_last = b_col[L - 1:L, :]
    w_end = jnp.exp(b_last - b_row + ig - m_new)
    w_end_col = jnp.sum(jnp.where(eye, w_end, 0.0), axis=1, keepdims=True)
    decay = jnp.exp(b_last + m_prev - m_new)
    kw = kf * w_end_col
    c_ref[0, hd] = decay * c_prev + lax.dot_general(
        kw.astype(BF16), vb, (((0,), (0,)), ((), ())), preferred_element_type=F32)
    n_ref[0, hd] = decay * n_prev + jnp.sum(kw, axis=0, keepdims=True)
    m_ref[0, hd] = m_new

    hn = h * lax.rsqrt(jnp.mean(h * h, axis=-1, keepdims=True) + LN_EPS) * ng_ref[:, vs]
    h_ref[:, vs] = (jax.nn.sigmoid(o_ref[:, vs]) * hn).astype(h_ref.dtype)


def mlstm(proj, gates, gate_b, norm_g, state, layer, *, B, S, L, nvalid, out_cols):
    nc = S // L
    H, dk, dv = M_HEADS, M_DK, M_DV
    if state is None:
        c0 = jnp.zeros((1, B, H, dk, dv), F32)
        n0 = jnp.zeros((1, B, H, 1, dk), F32)
        m0 = jnp.zeros((1, B, H, 1, 1), F32)
        layer = 0
    else:
        c0 = state[0]
        n0 = state[1].reshape(state[1].shape[0], B, H, 1, dk)
        m0 = state[2].reshape(state[2].shape[0], B, H, 1, 1)
    st_in = lambda b, c: (layer, b, 0, 0, 0)
    st4 = lambda b, c: (b, 0, 0, 0)
    row = lambda b, c: b * nc + c
    return pl.pallas_call(
        functools.partial(_mlstm_kernel, L=L, nvalid=nvalid),
        grid=(B, nc),
        in_specs=[
            pl.BlockSpec(memory_space=pltpu.SMEM),
            pl.BlockSpec((1, 2, H, nc, L), lambda b, c: (b, 0, 0, 0, 0)),
            pl.BlockSpec((L, H * dk), lambda b, c: (row(b, c), 0)),
            pl.BlockSpec((L, H * dk), lambda b, c: (row(b, c), 1)),
            pl.BlockSpec((L, H * dv), lambda b, c: (row(b, c), 1)),
            pl.BlockSpec((L, H * dv), lambda b, c: (row(b, c), 2)),
            pl.BlockSpec((1, H * dv), lambda b, c: (0, 0)),
            pl.BlockSpec((None, 1, H, dk, dv), st_in),
            pl.BlockSpec((None, 1, H, 1, dk), st_in),
            pl.BlockSpec((None, 1, H, 1, 1), st_in),
        ],
        out_specs=(
            pl.BlockSpec((L, out_cols), lambda b, c: (row(b, c), 0)),
            pl.BlockSpec((1, H, dk, dv), st4),
            pl.BlockSpec((1, H, 1, dk), st4),
            pl.BlockSpec((1, H, 1, 1), st4),
        ),
        out_shape=(
            jax.ShapeDtypeStruct((B * S, out_cols), BF16),
            jax.ShapeDtypeStruct((B, H, dk, dv), F32),
            jax.ShapeDtypeStruct((B, H, 1, dk), F32),
            jax.ShapeDtypeStruct((B, H, 1, 1), F32),
        ),
        compiler_params=_cparams(("parallel", "arbitrary")),
        name="mlstm",
    )(gate_b, gates, proj, proj, proj, proj, norm_g.reshape(1, H * dv), c0, n0, m0)


def _attn_prompt_kernel(*refs, dil, heads, head0, first, last, with_prev):
    bias_ref, q_ref, kc_ref, vc_ref = refs[:4]
    refs = refs[4:]
    if with_prev:
        kp_ref, vp_ref = refs[:2]
        refs = refs[2:]
    if not first:
        num_in_ref, sm_in_ref = refs[:2]
        refs = refs[2:]
    if last:
        _, out_ref, o32_ref = refs
    else:
        num_out_ref, sm_out_ref = refs
    T = A_J
    lane = lax.broadcasted_iota(jnp.int32, (T, 128), 1)
    if with_prev:
        key_ok = (pl.program_id(2) > 0) | (lax.broadcasted_iota(jnp.int32, (T, 2 * T), 1) >= T)
    nt = (((1,), (1,)), ((), ()))
    for r in range(dil):
        rows = pl.ds(r, T, stride=dil) if dil > 1 else pl.ds(0, T)
        sm_old = None if first else sm_in_ref[rows, :]
        sm_new = jnp.zeros((T, 128), F32)
        for hl in range(heads):
            hg = head0 + hl
            qh = (q_ref[hl, rows, :] * (A_DH ** -0.5)).astype(BF16)
            if with_prev:
                kk = jnp.concatenate([kp_ref[hl, rows, :], kc_ref[hl, rows, :]], axis=0).astype(BF16)
                vv = jnp.concatenate([vp_ref[hl, rows, :], vc_ref[hl, rows, :]], axis=0).astype(BF16)
                logits = lax.dot_general(qh, kk, nt, preferred_element_type=F32) + bias_ref[hg]
                logits = jnp.where(key_ok, logits, NEG_INF)
            else:
                kk = kc_ref[hl, rows, :].astype(BF16)
                vv = vc_ref[hl, rows, :].astype(BF16)
                logits = lax.dot_general(qh, kk, nt, preferred_element_type=F32) + bias_ref[hg]
            mx = jnp.max(logits, axis=1, keepdims=True)
            p = jnp.exp(logits - mx)
            s = jnp.sum(p, axis=1, keepdims=True)
            o = jnp.dot(p.astype(BF16), vv, preferred_element_type=F32)
            if not first:
                s_old = sm_old[:, hg:hg + 1]
                m_old = sm_old[:, A_HEADS + hg:A_HEADS + hg + 1]
                m_all = jnp.maximum(m_old, mx)
                a_old = jnp.exp(m_old - m_all)
                a_new = jnp.exp(mx - m_all)
                o = a_old * num_in_ref[hl, rows, :] + a_new * o
                s = a_old * s_old + a_new * s
                mx = m_all
            if last:
                o32_ref[hl, rows, :] = o / s
            else:
                num_out_ref[hl, rows, :] = o
                sm_new = jnp.where(lane == hg, s, sm_new)
                sm_new = jnp.where(lane == A_HEADS + hg, mx, sm_new)
        if not last:
            sm_out_ref[rows, :] = sm_new
    if last:
        for hl in range(heads):
            out_ref[:, hl * A_DH:(hl + 1) * A_DH] = o32_ref[hl].astype(out_ref.dtype)


def attn_prompt_branch(att, bias_c, bias_p, state, mix, *, B, S, g, dil, hb):
    first, last = state is None, mix is not None
    span = A_J * dil
    nb = S // span
    with_prev = nb > 1
    nhb = A_HEADS // hb
    assert nhb == 1 or last
    cur = lambda part: pl.BlockSpec((hb, span, A_DH), lambda b, c, n: ((3 * g + part) * nhb + c, b * nb + n, 0))
    prev = lambda part: pl.BlockSpec(
        (hb, span, A_DH), lambda b, c, n: ((3 * g + part) * nhb + c, jnp.maximum(b * nb + n - 1, 0), 0))
    bias = jnp.concatenate([bias_p, bias_c], axis=2) if with_prev else bias_c
    num_spec = pl.BlockSpec((hb, span, A_DH), lambda b, c, n: (c, b * nb + n, 0))
    sm_spec = pl.BlockSpec((span, 128), lambda b, c, n: (b * nb + n, 0))
    in_specs = [pl.BlockSpec(bias.shape, lambda b, c, n: (0, 0, 0)), cur(0), cur(1), cur(2)]
    args = [bias, att, att, att]
    if with_prev:
        in_specs += [prev(1), prev(2)]
        args += [att, att]
    if not first:
        in_specs += [num_spec, sm_spec]
        args += list(state)
    aliases = {}
    if last:
        in_specs.append(pl.BlockSpec(memory_space=pl.ANY))
        args.append(mix)
        aliases = {len(args) - 1: 0}
        out_specs = pl.BlockSpec((span, hb * A_DH), lambda b, c, n: (b * nb + n, MIX_ATT_COL * nhb + c))
        out_shape = jax.ShapeDtypeStruct(mix.shape, mix.dtype)
    else:
        out_specs = (num_spec, sm_spec)
        out_shape = (jax.ShapeDtypeStruct((A_HEADS, B * S, A_DH), F32), jax.ShapeDtypeStruct((B * S, 128), F32))

    def body(*refs):
        kw = dict(dil=dil, heads=hb, first=first, last=last, with_prev=with_prev)
        if nhb == 1:
            _attn_prompt_kernel(*refs, head0=0, **kw)
        else:
            for cb in range(nhb):
                @pl.when(pl.program_id(1) == cb)
                def _(cb=cb):
                    _attn_prompt_kernel(*refs, head0=cb * hb, **kw)

    return pl.pallas_call(
        body, grid=(B, nhb, nb), in_specs=in_specs, out_specs=out_specs, out_shape=out_shape,
        input_output_aliases=aliases,
        scratch_shapes=[pltpu.VMEM((hb, span, A_DH), F32)] if last else [],
        compiler_params=_cparams(("parallel", "parallel", "arbitrary")),
        name=f"attn_prompt_{g}",
    )(*args)


def _attn_decode_kernel(bias_ref, a_ref, k1_ref, v1_ref, k2_ref, v2_ref, k3_ref, v3_ref, out_ref):
    caches = ((k1_ref, v1_ref), (k2_ref, v2_ref), (k3_ref, v3_ref))
    nt = (((1,), (1,)), ((), ()))
    hw = A_HEADS * A_DH
    for h in range(A_HEADS):
        outs, dens, maxs = [], [], []
        for g in range(3):
            base = 3 * g * hw + h * A_DH
            qf = (a_ref[0, :, base:base + A_DH] * (A_DH ** -0.5)).astype(BF16)
            kn = a_ref[0, :, base + hw:base + hw + A_DH].astype(BF16).astype(F32)
            vn = a_ref[0, :, base + 2 * hw:base + 2 * hw + A_DH].astype(BF16).astype(F32)
            kc = caches[g][0][:, h, :].astype(BF16)
            vc = caches[g][1][:, h, :].astype(BF16)
            q16 = jnp.broadcast_to(qf, (16, A_DH))
            lc = lax.dot_general(q16, kc, nt, preferred_element_type=F32)[0:1, :] + bias_ref[g, h:h + 1, 0:A_J]
            ln = (jnp.sum(qf.astype(F32) * kn, axis=1, keepdims=True)
                  + bias_ref[g, h:h + 1, A_J:A_J + 1])
            mx = jnp.maximum(jnp.max(lc, axis=1, keepdims=True), ln)
            pc = jnp.exp(lc - mx)
            pn = jnp.exp(ln - mx)
            p16 = jnp.broadcast_to(pc.astype(BF16), (16, A_J))
            o = jnp.dot(p16, vc, preferred_element_type=F32)[0:1, :] + pn.astype(BF16).astype(F32) * vn
            outs.append(o)
            dens.append(jnp.sum(pc, axis=1, keepdims=True) + pn)
            maxs.append(mx)
        m_all = jnp.maximum(jnp.maximum(maxs[0], maxs[1]), maxs[2])
        wts = [jnp.exp(m - m_all) for m in maxs]
        num = wts[0] * outs[0] + wts[1] * outs[1] + wts[2] * outs[2]
        den = wts[0] * dens[0] + wts[1] * dens[1] + wts[2] * dens[2]
        out_ref[0, :, h * A_DH:(h + 1) * A_DH] = num / den


def attn_decode(att, caches, layer, bias_dec):
    B = att.shape[0]
    hw = A_HEADS * A_DH
    in_specs = [pl.BlockSpec(bias_dec.shape, lambda b: (0, 0, 0)),
                pl.BlockSpec((1, 1, N_ATT), lambda b: (b, 0, 0))]
    args = [bias_dec, att.reshape(B, 1, N_ATT)]
    for g, (window, dil) in enumerate(A_BRANCHES):
        for t in caches[2 * g:2 * g + 2]:
            assert t.shape[2] == window
            in_specs.append(pl.BlockSpec((None, None, A_J, None, A_HEADS, A_DH), lambda b: (layer, b, 0, 0, 0, 0)))
            args.append(t.reshape(t.shape[0], B, window // dil, dil, A_HEADS, A_DH))
    out = pl.pallas_call(
        _attn_decode_kernel, grid=(B,), in_specs=in_specs,
        out_specs=pl.BlockSpec((1, 1, hw), lambda b: (b, 0, 0)),
        out_shape=jax.ShapeDtypeStruct((B, 1, hw), F32),
        compiler_params=_cparams(("parallel",)),
        name="attn_decode",
    )(*args)
    return out.reshape(B, hw)


def _shift_kernel(c_ref, new_ref, o_ref):
    P = c_ref.shape[0]
    R = min(COPY_ROWS, P)

    def chunk(i, carry):
        o_ref[pl.ds(i * R, R)] = c_ref[pl.ds(i * R + 1, R)]
        return carry
    lax.fori_loop(0, P // R - 1, chunk, 0)
    o_ref[P - R:P - 1] = c_ref[P - R + 1:P]
    o_ref[P - 1:P] = new_ref[...]


def shift_append(cache, new_rows):
    Dp, B, P, H, Dh = cache.shape
    blk = lambda rows: pl.BlockSpec((None, None, rows, H, Dh), lambda l, b: (l, b, 0, 0, 0))
    return pl.pallas_call(
        _shift_kernel, grid=(Dp, B), in_specs=[blk(P), blk(1)], out_specs=blk(P),
        out_shape=jax.ShapeDtypeStruct(cache.shape, cache.dtype),
        compiler_params=_cparams(("parallel", "parallel")),
        name="shift_append",
    )(cache, new_rows.astype(cache.dtype))


def _conv_kernel(cv_ref, cg_ref, hv_ref, hg_ref, past_ref, cw_ref, cb_ref, lg_ref, lb_ref, *rest,
                 TS, RC, nvalid):
    y_ref, tail_ref, xs_ref, sh_ref = rest[-4:]
    t = pl.program_id(1)
    H = CONV_HALO

    @pl.when(t == 0)
    def _():
        xs_ref[0:H, :] = past_ref[0]

    @pl.when(t > 0)
    def _():
        xs_ref[0:H, :] = hv_ref[...] * jax.nn.sigmoid(hg_ref[...])

    xs_ref[H:H + TS, :] = cv_ref[...] * jax.nn.sigmoid(cg_ref[...])
    lead = H - (C_WIDTH - 1)
    SUB = SUBLANES
    span = sh_ref.shape[1]
    for p in range(1, SUB):
        sh_ref[p - 1] = xs_ref[p:p + span, :]
    for rc in range(TS // RC):
        acc = jnp.zeros((RC, C_CH), F32)
        for w in range(C_WIDTH):
            m, p = divmod(lead + w, SUB)
            r0 = rc * RC + SUB * m
            rows = xs_ref[r0:r0 + RC, :] if p == 0 else sh_ref[p - 1, r0:r0 + RC, :]
            acc = acc + rows * cw_ref[w:w + 1, :]
        y = _layer_norm_rows(acc + cb_ref[...], lg_ref[...], lb_ref[...])
        y_ref[rc * RC:(rc + 1) * RC, :] = (y * jax.nn.sigmoid(y)).astype(y_ref.dtype)
    tail_ref[0] = xs_ref[nvalid:nvalid + H, :]


def conv_block(proj, past, conv_w, conv_b, ln_g, ln_b, mix, *, B, S, TS, nvalid, vcol):
    nt = S // TS
    H = CONV_HALO
    RC = min(TS, 32)
    past_p = jnp.pad(past.astype(F32), ((0, 0), (H - (C_WIDTH - 1), 0), (0, 0)))
    cur = lambda col: pl.BlockSpec((TS, C_CH), lambda b, t: (b * nt + t, col))
    halo_rows = min(H, B * S)
    halo = lambda col: pl.BlockSpec(
        (halo_rows, C_CH), lambda b, t: (jnp.maximum((b * S + t * TS) // H - 1, 0), col))
    vec = pl.BlockSpec((1, C_CH), lambda b, t: (0, 0))
    in_specs = [cur(vcol), cur(vcol + 1), halo(vcol), halo(vcol + 1),
                pl.BlockSpec((1, H, C_CH), lambda b, t: (b, 0, 0)),
                pl.BlockSpec((H, C_CH), lambda b, t: (0, 0)), vec, vec, vec]
    args = [proj, proj, proj, proj, past_p, jnp.pad(conv_w.astype(F32), ((0, H - C_WIDTH), (0, 0))),
            conv_b.reshape(1, C_CH), ln_g.reshape(1, C_CH), ln_b.reshape(1, C_CH)]
    aliases = {}
    if mix is None:
        y_spec = pl.BlockSpec((TS, C_CH), lambda b, t: (b * nt + t, 0))
        y_shape = jax.ShapeDtypeStruct((B * S, C_CH), BF16)
    else:
        in_specs.append(pl.BlockSpec(memory_space=pl.ANY))
        args.append(mix)
        aliases = {len(args) - 1: 0}
        y_spec = pl.BlockSpec((TS, C_CH), lambda b, t: (b * nt + t, MIX_CONV_COL))
        y_shape = jax.ShapeDtypeStruct(mix.shape, mix.dtype)
    y, tail = pl.pallas_call(
        functools.partial(_conv_kernel, TS=TS, RC=RC, nvalid=nvalid),
        grid=(B, nt), in_specs=in_specs,
        out_specs=(y_spec, pl.BlockSpec((1, H, C_CH), lambda b, t: (b, 0, 0))),
        out_shape=(y_shape, jax.ShapeDtypeStruct((B, H, C_CH), F32)),
        scratch_shapes=[pltpu.VMEM((H + TS, C_CH), F32),
                        pltpu.VMEM((SUBLANES - 1, H + TS - SUBLANES, C_CH), F32)],
        input_output_aliases=aliases,
        compiler_params=_cparams(("parallel", "arbitrary")),
        name="conv_ln_silu",
    )(*args)
    return y, tail[:, H - (C_WIDTH - 1):]


def _xattn_kernel(q_ref, mk_ref, mv_ref, o_ref):
    q = q_ref[0]
    logits = lax.dot_general(q, mk_ref[...].astype(BF16), (((1,), (1,)), ((), ())),
                             preferred_element_type=F32)
    e = jnp.exp(logits - jnp.max(logits, axis=-1, keepdims=True))
    p = e / jnp.sum(e, axis=-1, keepdims=True)
    o_ref[0] = jnp.dot(p.astype(BF16), mv_ref[...].astype(BF16), preferred_element_type=F32).astype(o_ref.dtype)


def cross_attn(q, mk, mv, layer, *, TS):
    B, S, _ = q.shape
    TS = min(TS, S)
    mem = pl.BlockSpec((None, None, N_MEM, XA_DH), lambda b, h, t: (layer, b, 0, h))
    return pl.pallas_call(
        _xattn_kernel, grid=(B, XA_HEADS, S // TS),
        in_specs=[pl.BlockSpec((1, TS, XA_DH), lambda b, h, t: (b, t, h)), mem, mem],
        out_specs=pl.BlockSpec((1, TS, XA_DH), lambda b, h, t: (b, t, h)),
        out_shape=jax.ShapeDtypeStruct(q.shape, BF16),
        compiler_params=_cparams(("parallel", "parallel", "arbitrary")),
        name="cross_attn",
    )(q, mk, mv)


def _rel_bucket(dist):
    exact = N_BUCKETS // 2
    lg = jnp.log(jnp.maximum(dist, 1).astype(F32) / exact) / math.log(REL_MAX_DIST / exact)
    large = jnp.minimum(exact + (lg * (N_BUCKETS - exact)).astype(jnp.int32), N_BUCKETS - 1)
    return jnp.where(dist < exact, dist, large)


def _bias_tables(rel_bias):
    J = A_J
    cur, prev, dec = [], [], []
    for g, (window, dil) in enumerate(A_BRANCHES):
        table = rel_bias[:, g * A_HEADS:(g + 1) * A_HEADS].astype(F32)
        bucket = _rel_bucket(dil * jnp.arange(window // dil + 1))
        onehot = bucket[:, None] == jnp.arange(N_BUCKETS)[None, :]
        bias_j = jnp.sum(jnp.where(onehot[:, :, None], table[None], 0.0), axis=1)
        rev = bias_j.T[:, ::-1]
        u = jnp.concatenate([rev, jnp.full((A_HEADS, J), NEG_INF, F32)], axis=1)
        tile = jnp.broadcast_to(u[:, None, :], (A_HEADS, J, 2 * J + 1)).reshape(A_HEADS, J * (2 * J + 1))
        tile = tile[:, :J * 2 * J].reshape(A_HEADS, J, 2 * J)
        prev.append(tile[:, :, :J])
        cur.append(tile[:, :, J:])
        dec.append(jnp.pad(rev, ((0, 0), (0, 2 * J - (J + 1)))))
    return cur, prev, jnp.stack(dec)


def _prep_weights(w_in, w_out, xa_wq, xa_wk, xa_wv, xa_wo, ffn_w1, ffn_w2):
    a0 = N_MAIN + N_GATE
    w_gate = jnp.pad(w_in[:, :, N_MAIN:a0], ((0, 0), (0, 0), (0, GATE_PAD - N_GATE))).astype(BF16)
    w_main = jnp.concatenate([w_in[:, :, :N_MAIN], w_in[:, :, a0 + N_ATT:]], axis=2).astype(BF16)
    w_att = w_in[:, :, a0:a0 + N_ATT].astype(BF16)
    cast = lambda w: w.astype(BF16)
    return dict(main=w_main, att=w_att, gate=w_gate,
                out=cast(w_out), wq=cast(xa_wq), wk=cast(xa_wk), wv=cast(xa_wv),
                wo=cast(xa_wo), w1=cast(ffn_w1), w2=cast(ffn_w2))


def _gate_layout(gmat, B, S, L):
    g = gmat[:, :N_GATE].reshape(B, S // L, L, 2, M_HEADS)
    return g.transpose(0, 3, 4, 1, 2)


def _trunk_tail(xp, xs, mix, smix, mk, mv, smk, smv, w, l, lg, lb, *, B, S, BS):
    mm = functools.partial(matmul, tm=1024, tn=1024, tk=D_MODEL)
    MS = xs[0].shape[0]

    def both_ln(sub, ssub, i):
        return post_ln(xp[0], sub, lg[i], lb[i], tr=256), post_ln(xs[0], ssub, lg[i], lb[i], tr=MS)

    xp, xs = both_ln(*mm(mix, w["out"], l, rider=smix), 0)
    q, sq = mm(xp[1], w["wq"], l, out_dtype=BF16, scale=XA_DH ** -0.5, rider=xs[1])
    o = cross_attn(q.reshape(B, S, D_MODEL), mk, mv, 0, TS=512).reshape(B * S, D_MODEL)
    sq = jnp.zeros((BS, MS, D_MODEL), BF16).at[:, 0].set(sq[:BS])
    so = jnp.pad(cross_attn(sq, smk, smv, l, TS=MS)[:, 0], ((0, MS - BS), (0, 0)))
    xp, xs = both_ln(*mm(o, w["wo"], l, rider=so), 1)
    hid, shid = mm(xp[1], w["w1"], l, out_dtype=BF16, epilogue="relu2", rider=xs[1])
    xp, xs = both_ln(*mm(hid, w["w2"], l, rider=shid), 2)
    return xp, xs


def kernel(x_prompt, x_sample, mem_prompt, cache_win_k1, cache_win_v1, cache_win_k2, cache_win_v2, cache_win_k3, cache_win_v3, state_mlstm_C, state_mlstm_n, state_mlstm_m, state_conv, cache_mem_k, cache_mem_v, rel_bias, w_in, mlstm_gate_bias, mlstm_norm_g, conv_w, conv_b, conv_ln_g, conv_ln_b, w_out, xa_wq, xa_wk, xa_wv, xa_wo, ffn_w1, ffn_w2, ln_g, ln_b):
    B, S, D = x_prompt.shape
    BS = x_sample.shape[0]
    MS = SAMPLE_ROWS
    L = math.gcd(S, M_CHUNK)
    bias_c, bias_p, bias_dec = _bias_tables(rel_bias)
    cache_win = (cache_win_k1, cache_win_v1, cache_win_k2, cache_win_v2, cache_win_k3, cache_win_v3)
    sample_state = (state_mlstm_C.astype(F32), state_mlstm_n.astype(F32), state_mlstm_m.astype(F32))
    smem_k = cache_mem_k.reshape(DEPTH, BS, N_MEM, D)
    smem_v = cache_mem_v.reshape(DEPTH, BS, N_MEM, D)

    xp32 = x_prompt.reshape(B * S, D)
    xp16 = xp32.astype(BF16)
    xs32 = jnp.pad(x_sample.reshape(BS, D), ((0, MS - BS), (0, 0)))
    xs16 = xs32.astype(BF16)
    mem16 = mem_prompt.reshape(B * N_MEM, D).astype(BF16)

    p_win = [[] for _ in range(6)]
    new_rows = [[] for _ in range(6)]
    p_C, p_n, p_m, p_conv, p_mk, p_mv = [], [], [], [], [], []
    s_C, s_n, s_m, s_conv = [], [], [], []
    w = _prep_weights(w_in, w_out, xa_wq, xa_wk, xa_wv, xa_wo, ffn_w1, ffn_w2)
    mm = functools.partial(matmul, tm=1024, tn=1024, tk=D)
    xp, xs = (xp32, xp16), (xs32, xs16)
    for l in range(DEPTH):
        gate_b = mlstm_gate_bias[l].astype(F32)
        main, smain = mm(xp[1], w["main"], l, rider=xs[1])
        att, satt = mm(xp[1], w["att"], l, head_major=True, rider=xs[1])
        gmat, sgm = mm(xp[1], w["gate"], l, rider=xs[1])
        satt = satt[:BS]

        mix, C1, n1, m1 = mlstm(main, _gate_layout(gmat, B, S, L), gate_b, mlstm_norm_g[l], None, 0,
                                B=B, S=S, L=L, nvalid=L, out_cols=D)
        state = None
        for g, (window, dil) in enumerate(A_BRANCHES):
            last = g == len(A_BRANCHES) - 1
            out = attn_prompt_branch(att, bias_c[g], bias_p[g], state, mix if last else None,
                                     B=B, S=S, g=g, dil=dil, hb=A_HEADS // 2 if last else A_HEADS)
            if last:
                mix = out
            else:
                state = out
            keep = min(window, S)
            for j in range(2):
                h0 = (3 * g + 1 + j) * A_HEADS
                kv = att[h0:h0 + A_HEADS].reshape(A_HEADS, B, S, A_DH)[:, :, S - keep:]
                p_win[2 * g + j].append(kv.transpose(1, 2, 0, 3))
        mix, conv_tail = conv_block(main, jnp.zeros((B, C_WIDTH - 1, C_CH), F32), conv_w[l], conv_b[l],
                                    conv_ln_g[l], conv_ln_b[l], mix, B=B, S=S, TS=128, nvalid=128, vcol=CONV_COL)
        mk = mm(mem16, w["wk"], l)
        mv = mm(mem16, w["wv"], l)

        LS = M_CHUNK
        main_pad = jnp.zeros((BS, LS, N_MAIN), F32).at[:, 0].set(smain[:BS, :N_MAIN]).reshape(BS * LS, N_MAIN)
        gate_pad = jnp.zeros((BS, LS, GATE_PAD), F32).at[:, 0].set(sgm[:BS]).reshape(BS * LS, GATE_PAD)
        shm, Cs, ns, ms = mlstm(main_pad, _gate_layout(gate_pad, BS, LS, LS), gate_b, mlstm_norm_g[l],
                                sample_state, l, B=BS, S=LS, L=LS, nvalid=1, out_cols=M_HEADS * M_DV)
        shm = shm.reshape(BS, LS, M_HEADS * M_DV)[:, 0]
        sha = attn_decode(satt, cache_win, l, bias_dec)
        for g in range(len(A_BRANCHES)):
            for j in range(2):
                c0 = (3 * g + 1 + j) * COL_BLK
                new_rows[2 * g + j].append(satt[:, c0:c0 + COL_BLK].reshape(BS, 1, A_HEADS, A_DH))
        CT = 8
        cpad = jnp.zeros((BS, CT, 2 * C_CH), F32).at[:, 0].set(smain[:BS, N_MAIN:]).reshape(BS * CT, 2 * C_CH)
        syc, sconv_tail = conv_block(cpad, state_conv[l], conv_w[l], conv_b[l], conv_ln_g[l], conv_ln_b[l], None,
                                     B=BS, S=CT, TS=CT, nvalid=1, vcol=0)
        syc = syc.reshape(BS, CT, C_CH)[:, 0]
        smix = jnp.concatenate([shm, sha.astype(BF16), syc], axis=1)
        smix = jnp.pad(smix, ((0, MS - BS), (0, 0)))

        xp, xs = _trunk_tail(xp, xs, mix, smix, mk.reshape(1, B, N_MEM, D), mv.reshape(1, B, N_MEM, D),
                             smem_k, smem_v, w, l, ln_g[l], ln_b[l], B=B, S=S, BS=BS)
        p_C.append(C1); p_n.append(n1.reshape(B, M_HEADS, M_DK)); p_m.append(m1.reshape(B, M_HEADS))
        p_conv.append(conv_tail)
        p_mk.append(mk.reshape(B, N_MEM, XA_HEADS, XA_DH)); p_mv.append(mv.reshape(B, N_MEM, XA_HEADS, XA_DH))
        s_C.append(Cs); s_n.append(ns.reshape(BS, M_HEADS, M_DK)); s_m.append(ms.reshape(BS, M_HEADS))
        s_conv.append(sconv_tail)
    xp32, xs32 = xp[0], xs[0]

    st = lambda xs: jnp.stack(xs)
    s_win = [shift_append(cache_win[i], st(new_rows[i])) for i in range(6)]
    return (xp32.reshape(B, S, D), xs32[:BS].reshape(BS, 1, D),
            st(p_win[0]), st(p_win[1]), st(p_win[2]), st(p_win[3]), st(p_win[4]), st(p_win[5]),
            st(p_C), st(p_n), st(p_m), st(p_conv), st(p_mk), st(p_mv),
            s_win[0], s_win[1], s_win[2], s_win[3], s_win[4], s_win[5],
            st(s_C), st(s_n), st(s_m), st(s_conv))
```
